```python
import math
import jax, jax.numpy as jnp
from jax import lax
import numpy as np

D_MODEL = 1024
BATCH = 16
SEQ = 4096
DEPTH = 1

RWKV_HEADS = 8
RWKV_HEAD_DIM = 64
RWKV_DIM = RWKV_HEADS * RWKV_HEAD_DIM
DECAY_LORA = 64
AAA_LORA = 64
GATE_LORA = 128
GN_EPS = 64e-5
MLA_HEADS = 8
QK_NOPE_DIM = 64
QK_ROPE_DIM = 32
QK_HEAD_DIM = QK_NOPE_DIM + QK_ROPE_DIM
V_HEAD_DIM = 64
MLA_DIM = MLA_HEADS * V_HEAD_DIM
Q_LORA = 512
KV_LORA = 256
ROPE_THETA = 10000.0
Q_BLOCK = 128
D_FF = int(math.ceil(8 * D_MODEL / 3 / 256)) * 256
LN_EPS = 1e-5
RMS_EPS = 1e-6
ALPHA = (2.0 * DEPTH) ** 0.25
BETA = (8.0 * DEPTH) ** -0.25
RWKV_SPLITS = (RWKV_DIM, RWKV_DIM, RWKV_DIM, DECAY_LORA, AAA_LORA, GATE_LORA)
SHIFT_COLS = 3 * RWKV_DIM + DECAY_LORA + AAA_LORA + GATE_LORA
REST_SPLITS = (Q_LORA, KV_LORA, QK_ROPE_DIM, D_MODEL, D_MODEL)
D_IN = SHIFT_COLS + Q_LORA + KV_LORA + QK_ROPE_DIM + 2 * D_MODEL

kernel_name = "hybrid_rwkv7_mla_gated_deepnorm"


def _split(z, sizes):
    outs, start = [], 0
    for s in sizes:
        outs.append(z[..., start:start + s])
        start += s
    return outs


def layer_norm(x, g, b, eps=LN_EPS):
    xf = x.astype(jnp.float32)
    mu = jnp.mean(xf, axis=-1, keepdims=True)
    var = jnp.mean(jnp.square(xf - mu), axis=-1, keepdims=True)
    return ((xf - mu) * lax.rsqrt(var + eps) * g.astype(jnp.float32) + b.astype(jnp.float32)).astype(x.dtype)


def rms_norm(x, g, eps=RMS_EPS):
    xf = x.astype(jnp.float32)
    ms = jnp.mean(jnp.square(xf), axis=-1, keepdims=True)
    return (xf * lax.rsqrt(ms + eps) * g.astype(jnp.float32)).astype(x.dtype)


def rope_cos_sin(positions):
    inv_freq = ROPE_THETA ** (-jnp.arange(0, QK_ROPE_DIM, 2, dtype=jnp.float32) / QK_ROPE_DIM)
    ang = positions.astype(jnp.float32)[..., None] * inv_freq
    return jnp.cos(ang), jnp.sin(ang)


def apply_rope(x, cos, sin):
    half = x.shape[-1] // 2
    x1, x2 = x[..., :half], x[..., half:]
    cos = cos.astype(x.dtype)
    sin = sin.astype(x.dtype)
    return jnp.concatenate([x1 * cos - x2 * sin, x2 * cos + x1 * sin], axis=-1)


def rwkv7_scan(r, w, k, v, a, b):
    B, S, H, N = r.shape
    to_time = lambda t: jnp.moveaxis(t.astype(jnp.float32), 1, 0)
    xs = tuple(to_time(t) for t in (r, w, k, v, a, b))

    def step(state, inp):
        r_t, w_t, k_t, v_t, a_t, b_t = inp
        sa = jnp.einsum('bhvk,bhk->bhv', state, a_t)
        state = state * w_t[:, :, None, :] + sa[..., None] * b_t[:, :, None, :] + v_t[..., None] * k_t[:, :, None, :]
        y = jnp.einsum('bhvk,bhk->bhv', state, r_t)
        return state, y

    state0 = jnp.zeros((B, H, N, N), jnp.float32)
    _, ys = lax.scan(step, state0, xs)
    return jnp.moveaxis(ys, 0, 1).astype(r.dtype)


def rwkv7_branch(z_r, z_k, z_v, z_wd, z_ad, z_gd, w_decay_up, w_decay_base, w_aaa_up, w_aaa_base,
                 w_gate_up, k_k, k_a, r_k, lnx_g, lnx_b):
    B, S, _ = z_r.shape
    H, N = RWKV_HEADS, RWKV_HEAD_DIM
    w_log = -jax.nn.softplus(-(w_decay_base + jnp.tanh(z_wd) @ w_decay_up)) - 0.5
    decay = jnp.exp(-jnp.exp(w_log.astype(jnp.float32)))
    a = jax.nn.sigmoid(w_aaa_base + z_ad @ w_aaa_up)
    g = jax.nn.sigmoid(z_gd) @ w_gate_up
    kk = (z_k * k_k).reshape(B, S, H, N)
    kk_f = kk.astype(jnp.float32)
    kk = (kk_f / jnp.maximum(jnp.sqrt(jnp.sum(kk_f * kk_f, axis=-1, keepdims=True)), 1e-12)).astype(z_k.dtype)
    k = z_k * (1.0 + (a - 1.0) * k_a)
    heads = lambda t: t.reshape(B, S, H, N)
    r, k, v, a_h, decay = heads(z_r), heads(k), heads(z_v), heads(a), heads(decay)
    y = rwkv7_scan(r, decay, k, v, -kk, kk * a_h)
    y = layer_norm(y, lnx_g.reshape(H, N), lnx_b.reshape(H, N), eps=GN_EPS)
    bonus = jnp.sum(r * k * r_k, axis=-1, keepdims=True) * v
    return (y + bonus).reshape(B, S, RWKV_DIM) * g


def mla_branch(c_q, c_kv, k_pe_raw, positions, q_norm_g, w_uq, kv_norm_g, w_ukv):
    B, S, _ = c_q.shape
    H = MLA_HEADS
    cos, sin = rope_cos_sin(positions)
    q = (rms_norm(c_q, q_norm_g) @ w_uq).reshape(B, S, H, QK_HEAD_DIM)
    q_nope, q_pe = q[..., :QK_NOPE_DIM], q[..., QK_NOPE_DIM:]
    q_pe = apply_rope(q_pe, cos[:, :, None, :], sin[:, :, None, :])
    kv = (rms_norm(c_kv, kv_norm_g) @ w_ukv).reshape(B, S, H, QK_NOPE_DIM + V_HEAD_DIM)
    k_nope, v = kv[..., :QK_NOPE_DIM], kv[..., QK_NOPE_DIM:]
    k_pe = apply_rope(k_pe_raw, cos, sin)
    scale = QK_HEAD_DIM ** -0.5
    nb = S // Q_BLOCK
    to_blocks = lambda t: jnp.moveaxis(t.reshape(B, nb, Q_BLOCK, H, t.shape[-1]), 1, 0)
    key_idx = jnp.arange(S)

    def attend(args):
        qn, qp, blk = args
        s = jnp.einsum('bqhd,bkhd->bhqk', qn, k_nope) + jnp.einsum('bqhr,bkr->bhqk', qp, k_pe)
        s = s.astype(jnp.float32) * scale
        q_idx = blk * Q_BLOCK + jnp.arange(Q_BLOCK)
        mask = key_idx[None, :] <= q_idx[:, None]
        p = jax.nn.softmax(jnp.where(mask, s, -jnp.inf), axis=-1).astype(v.dtype)
        return jnp.einsum('bhqk,bkhd->bqhd', p, v)

    o = lax.map(attend, (to_blocks(q_nope), to_blocks(q_pe), jnp.arange(nb)))
    return jnp.moveaxis(o, 0, 1).reshape(B, S, MLA_DIM)


def hybrid_layer(x, positions, w_in, mu_shift, w_decay_up, w_decay_base, w_aaa_up, w_aaa_base, w_gate_up,
                 k_k, k_a, r_k, lnx_g, lnx_b, q_norm_g, w_uq, kv_norm_g, w_ukv, w_proj_rwkv, w_proj_mla,
                 w_out, ln1_g, ln1_b, w_ffn_gate, w_ffn_up, w_ffn_down, ln2_g, ln2_b):
    z = x @ w_in
    zs, zm = z[..., :SHIFT_COLS], z[..., SHIFT_COLS:]
    zs_prev = jnp.pad(zs, ((0, 0), (1, 0), (0, 0)))[:, :-1]
    zs = zs + mu_shift * (zs_prev - zs)
    z_r, z_k, z_v, z_wd, z_ad, z_gd = _split(zs, RWKV_SPLITS)
    c_q, c_kv, k_pe_raw, gate_r, gate_m = _split(zm, REST_SPLITS)
    y_r = rwkv7_branch(z_r, z_k, z_v, z_wd, z_ad, z_gd, w_decay_up, w_decay_base, w_aaa_up, w_aaa_base,
                       w_gate_up, k_k, k_a, r_k, lnx_g, lnx_b)
    y_m = mla_branch(c_q, c_kv, k_pe_raw, positions, q_norm_g, w_uq, kv_norm_g, w_ukv)
    merged = jax.nn.sigmoid(gate_r) * (y_r @ w_proj_rwkv) + jax.nn.sigmoid(gate_m) * (y_m @ w_proj_mla)
    h = layer_norm(ALPHA * x + merged @ w_out, ln1_g, ln1_b)
    ffn = (jax.nn.silu(h @ w_ffn_gate) * (h @ w_ffn_up)) @ w_ffn_down
    return layer_norm(ALPHA * h + ffn, ln2_g, ln2_b)


def setup_inputs(seed: int = 0) -> dict:
    key = jax.random.key(seed)
    ks = jax.random.split(key, 32)
    L = DEPTH
    f32 = jnp.float32

    def nrm(k, shape, scale):
        return jax.random.normal(k, shape, f32) * scale

    x = jax.random.normal(ks[0], (BATCH, SEQ, D_MODEL), f32)
    offset = jax.random.randint(ks[1], (BATCH, 1), 0, 1024, dtype=jnp.int32)
    positions = offset + jnp.arange(SEQ, dtype=jnp.int32)[None, :]
    col_scale = jnp.ones((D_IN,), f32).at[2 * RWKV_DIM:3 * RWKV_DIM].set(BETA)
    w_in = nrm(ks[2], (L, D_MODEL, D_IN), D_MODEL ** -0.5) * col_scale
    mu_shift = jax.random.uniform(ks[3], (L, SHIFT_COLS), f32)
    w_decay_up = nrm(ks[4], (L, DECAY_LORA, RWKV_DIM), 0.1 * DECAY_LORA ** -0.5)
    w_decay_base = jax.random.uniform(ks[5], (L, RWKV_DIM), f32, minval=-6.0, maxval=-1.0)
    w_aaa_up = nrm(ks[6], (L, AAA_LORA, RWKV_DIM), AAA_LORA ** -0.5)
    w_aaa_base = nrm(ks[7], (L, RWKV_DIM), 0.1)
    w_gate_up = nrm(ks[8], (L, GATE_LORA, RWKV_DIM), GATE_LORA ** -0.5)
    k_k = 0.85 + nrm(ks[9], (L, RWKV_DIM), 0.02)
    k_a = 1.0 + nrm(ks[10], (L, RWKV_DIM), 0.02)
    r_k = nrm(ks[11], (L, RWKV_HEADS, RWKV_HEAD_DIM), 0.1)
    lnx_g = 1.0 + nrm(ks[12], (L, RWKV_DIM), 0.02)
    lnx_b = nrm(ks[13], (L, RWKV_DIM), 0.02)
    q_norm_g = 1.0 + nrm(ks[14], (L, Q_LORA), 0.02)
    w_uq = nrm(ks[15], (L, Q_LORA, MLA_HEADS * QK_HEAD_DIM), Q_LORA ** -0.5)
    kv_norm_g = 1.0 + nrm(ks[16], (L, KV_LORA), 0.02)
    kv_col_scale = jnp.tile(jnp.concatenate([jnp.ones((QK_NOPE_DIM,), f32), jnp.full((V_HEAD_DIM,), BETA, f32)]), MLA_HEADS)
    w_ukv = nrm(ks[17], (L, KV_LORA, MLA_HEADS * (QK_NOPE_DIM + V_HEAD_DIM)), KV_LORA ** -0.5) * kv_col_scale
    w_proj_rwkv = nrm(ks[18], (L, RWKV_DIM, D_MODEL), BETA * RWKV_DIM ** -0.5)
    w_proj_mla = nrm(ks[19], (L, MLA_DIM, D_MODEL), BETA * MLA_DIM ** -0.5)
    w_out = nrm(ks[20], (L, D_MODEL, D_MODEL), BETA * D_MODEL ** -0.5)
    ln1_g = 1.0 + nrm(ks[21], (L, D_MODEL), 0.02)
    ln1_b = nrm(ks[22], (L, D_MODEL), 0.02)
    w_ffn_gate = nrm(ks[23], (L, D_MODEL, D_FF), BETA * D_MODEL ** -0.5)
    w_ffn_up = nrm(ks[24], (L, D_MODEL, D_FF), BETA * D_MODEL ** -0.5)
    w_ffn_down = nrm(ks[25], (L, D_FF, D_MODEL), BETA * D_FF ** -0.5)
    ln2_g = 1.0 + nrm(ks[26], (L, D_MODEL), 0.02)
    ln2_b = nrm(ks[27], (L, D_MODEL), 0.02)
    return {"x": x, "positions": positions, "w_in": w_in, "mu_shift": mu_shift,
            "w_decay_up": w_decay_up, "w_decay_base": w_decay_base, "w_aaa_up": w_aaa_up,
            "w_aaa_base": w_aaa_base, "w_gate_up": w_gate_up, "k_k": k_k, "k_a": k_a, "r_k": r_k,
            "lnx_g": lnx_g, "lnx_b": lnx_b, "q_norm_g": q_norm_g, "w_uq": w_uq,
            "kv_norm_g": kv_norm_g, "w_ukv": w_ukv, "w_proj_rwkv": w_proj_rwkv,
            "w_proj_mla": w_proj_mla, "w_out": w_out, "ln1_g": ln1_g, "ln1_b": ln1_b,
            "w_ffn_gate": w_ffn_gate, "w_ffn_up": w_ffn_up, "w_ffn_down": w_ffn_down,
            "ln2_g": ln2_g, "ln2_b": ln2_b}


def reference(x, positions, w_in, mu_shift, w_decay_up, w_decay_base, w_aaa_up, w_aaa_base, w_gate_up,
              k_k, k_a, r_k, lnx_g, lnx_b, q_norm_g, w_uq, kv_norm_g, w_ukv, w_proj_rwkv, w_proj_mla,
              w_out, ln1_g, ln1_b, w_ffn_gate, w_ffn_up, w_ffn_down, ln2_g, ln2_b):
    h = x
    for l in range(DEPTH):
        h = hybrid_layer(h, positions, w_in[l], mu_shift[l], w_decay_up[l], w_decay_base[l], w_aaa_up[l],
                         w_aaa_base[l], w_gate_up[l], k_k[l], k_a[l], r_k[l], lnx_g[l], lnx_b[l],
                         q_norm_g[l], w_uq[l], kv_norm_g[l], w_ukv[l], w_proj_rwkv[l], w_proj_mla[l],
                         w_out[l], ln1_g[l], ln1_b[l], w_ffn_gate[l], w_ffn_up[l], w_ffn_down[l],
                         ln2_g[l], ln2_b[l])
    return h
```

```python
import functools
import math

import jax
import jax.numpy as jnp
from jax import lax
from jax.experimental import pallas as pl
from jax.experimental.pallas import tpu as pltpu

F32 = jnp.float32
BF16 = jnp.bfloat16

HEAD_DIM = 64
ROPE_DIM = 32
HALF_ROPE = ROPE_DIM // 2
ROPE_THETA = 10000.0
GN_EPS = 64e-5
LN_EPS = 1e-5
RMS_EPS = 1e-6
LANES = 128
SCAN_CHUNK = 64
VMEM_LIMIT = 56 * 1024 * 1024
NEG_BIG = -1e30


def _dot(a, b):
    return jnp.dot(a, b, preferred_element_type=F32)


def _dot_nt(a, b):
    return lax.dot_general(a, b, (((1,), (1,)), ((), ())), preferred_element_type=F32)


def _dot_tn(a, b):
    return lax.dot_general(a, b, (((0,), (0,)), ((), ())), preferred_element_type=F32)


def _split3(x):
    hi = x.astype(BF16)
    r1 = x - hi.astype(F32)
    mid = r1.astype(BF16)
    lo = (r1 - mid.astype(F32)).astype(BF16)
    return hi, mid, lo


def _sum_right(x, m):
    hi = x.astype(BF16)
    lo = (x - hi.astype(F32)).astype(BF16)
    return _dot(hi, m) + _dot(lo, m)


def _sum_left(m, x):
    hi, mid, lo = _split3(x)
    return _dot(m, hi) + _dot(m, mid) + _dot(m, lo)


def _const_spec(shape):
    nd = len(shape)
    return pl.BlockSpec(shape, lambda *_: (0,) * nd, pipeline_mode=pl.Buffered(1))


def _params(*sem):
    return pltpu.CompilerParams(dimension_semantics=sem, vmem_limit_bytes=VMEM_LIMIT)


def _rwkv_prep_kernel(x_ref, ws_ref, mu_ref, wdu_ref, wau_ref, wgu_ref, dbase_ref, abase_ref, kk_ref, ka_ref,
                      rk_ref, hsum_ref, ctri_ref, call_ref, csel_ref,
                      rt_ref, at_ref, bt_ref, kt_ref, bh_ref, kh_ref, v_ref, gc_ref, bonus_ref, g_ref,
                      carry_ref, *, tiles_per_seq, rdim):
    tm = x_ref.shape[0]
    first = (pl.program_id(0) % tiles_per_seq) == 0

    @pl.when(first)
    def _():
        carry_ref[...] = jnp.zeros_like(carry_ref)

    z = _dot(x_ref[...].astype(BF16), ws_ref[...])
    prev_last = carry_ref[0:1, :]
    carry_ref[0:1, :] = z[tm - 1:tm, :]
    row = lax.broadcasted_iota(jnp.int32, (tm, 1), 0)
    z_prev = jnp.where(row == 0, prev_last, pltpu.roll(z, 1, axis=0))
    zs = z + mu_ref[...] * (z_prev - z)

    z_r = zs[:, 0:rdim]
    z_k = zs[:, rdim:2 * rdim]
    z_v = zs[:, 2 * rdim:3 * rdim]
    z_lora = zs[:, 3 * rdim:3 * rdim + LANES]
    z_gd = zs[:, 3 * rdim + LANES:3 * rdim + 2 * LANES]

    t = -(dbase_ref[...] + _dot(jnp.tanh(z_lora).astype(BF16), wdu_ref[...]))
    softplus = jnp.maximum(t, 0.0) + jnp.log1p(jnp.exp(-jnp.abs(t)))
    logw = -jnp.exp(-softplus - 0.5)
    a = jax.nn.sigmoid(abase_ref[...] + _dot(z_lora.astype(BF16), wau_ref[...]))
    g_ref[...] = _dot(jax.nn.sigmoid(z_gd).astype(BF16), wgu_ref[...])

    hsum = hsum_ref[...]
    kk = z_k * kk_ref[...]
    kk = kk / jnp.maximum(jnp.sqrt(_sum_right(kk * kk, hsum)), 1e-12)
    kmod = z_k * (1.0 + (a - 1.0) * ka_ref[...])
    b = kk * a
    bonus_ref[...] = _sum_right(z_r * kmod * rk_ref[...], hsum) * z_v

    cum = _sum_left(ctri_ref[...], logw)
    tot = _sum_left(call_ref[...], logw)
    gc_ref[...] = jnp.exp(_sum_left(csel_ref[...], logw)[0:gc_ref.shape[0], :])

    e_neg = jnp.exp(-cum)
    e_tail = jnp.exp(tot - cum)
    rt_ref[...] = (z_r * jnp.exp(cum)).astype(BF16)
    at_ref[...] = (-kk * jnp.exp(cum - logw)).astype(BF16)
    bt_ref[...] = (b * e_neg).astype(BF16)
    kt_ref[...] = (kmod * e_neg).astype(BF16)
    bh_ref[...] = (b * e_tail).astype(BF16)
    kh_ref[...] = (kmod * e_tail).astype(BF16)
    v_ref[...] = z_v.astype(BF16)


def _rwkv_prep(x2, ws, mu, wdu, wau, wgu, dbase, abase, k_k, k_a, r_k, hsum, seq, tm):
    T, D = x2.shape
    rdim = k_k.shape[-1]
    ncol = ws.shape[1]
    cpt = tm // SCAN_CHUNK
    r = jnp.arange(tm)
    same = (r[:, None] // SCAN_CHUNK) == (r[None, :] // SCAN_CHUNK)
    ctri = (same & (r[None, :] <= r[:, None])).astype(BF16)
    call = same.astype(BF16)
    csel_rows = max(cpt, 16)
    csel = ((jnp.arange(csel_rows)[:, None] == (r[None, :] // SCAN_CHUNK))).astype(BF16)

    row_spec = lambda w: pl.BlockSpec((tm, w), lambda i: (i, 0))
    slab = jax.ShapeDtypeStruct((T, rdim), BF16)
    slab32 = jax.ShapeDtypeStruct((T, rdim), F32)
    out_shape = (slab,) * 7 + (jax.ShapeDtypeStruct((T // SCAN_CHUNK, rdim), F32), slab32, slab32)
    out_specs = (row_spec(rdim),) * 7 + (pl.BlockSpec((cpt, rdim), lambda i: (i, 0)), row_spec(rdim), row_spec(rdim))
    consts = (ws, mu, wdu, wau, wgu, dbase, abase, k_k, k_a, r_k, hsum, ctri, call, csel)
    return pl.pallas_call(
        functools.partial(_rwkv_prep_kernel, tiles_per_seq=seq // tm, rdim=rdim),
        grid=(T // tm,),
        in_specs=[row_spec(D)] + [_const_spec(c.shape) for c in consts],
        out_specs=out_specs,
        out_shape=out_shape,
        scratch_shapes=[pltpu.VMEM((8, ncol), F32)],
        compiler_params=_params("arbitrary"),
        name="rwkv_prep",
    )(x2, *consts)


def _scan_kernel(rt_ref, at_ref, bt_ref, kt_ref, bh_ref, kh_ref, v_ref, gc_ref, y_ref, s_ref):
    C = SCAN_CHUNK
    npairs = s_ref.shape[0]

    @pl.when(pl.program_id(1) == 0)
    def _():
        s_ref[...] = jnp.zeros_like(s_ref)

    row = lax.broadcasted_iota(jnp.int32, (2 * C, 1), 0)
    lane = lax.broadcasted_iota(jnp.int32, (1, LANES), 1)
    keep = (row < C) == (lane < HEAD_DIM)
    t_idx = row % C
    s_idx = lane % C
    m_strict = keep & (s_idx < t_idx)
    m_incl = keep & (s_idx <= t_idx)
    eye = jnp.where(keep & (s_idx == t_idx), 1.0, 0.0)
    keep_b = jnp.where(keep, 1.0, 0.0).astype(BF16)

    def stacked(ref, sl):
        x = ref[:, sl]
        return jnp.concatenate([x, x], axis=0) * keep_b

    for p in range(npairs):
        sl = slice(p * LANES, (p + 1) * LANES)
        at_m, rt_m = stacked(at_ref, sl), stacked(rt_ref, sl)
        bt_m, kt_m = stacked(bt_ref, sl), stacked(kt_ref, sl)
        bh_m, kh_m = stacked(bh_ref, sl), stacked(kh_ref, sl)
        v_m = stacked(v_ref, sl)

        gram = _dot_nt(jnp.concatenate([at_m, rt_m], axis=0), jnp.concatenate([bt_m, kt_m], axis=0))
        a_ab = jnp.where(m_strict, gram[0:2 * C, 0:LANES], 0.0)
        a_ak = jnp.where(m_strict, gram[0:2 * C, LANES:2 * LANES], 0.0)
        a_rb = jnp.where(m_incl, gram[2 * C:4 * C, 0:LANES], 0.0)
        a_rk = jnp.where(m_incl, gram[2 * C:4 * C, LANES:2 * LANES], 0.0)

        inv = eye + a_ab
        pw = a_ab.astype(BF16)
        for _ in range(int(math.log2(C)) - 1):
            pw32 = _dot(pw, pw)
            pw = pw32.astype(BF16)
            inv = inv + _dot(inv.astype(BF16), pw)

        state = s_ref[p]
        state_b = state.astype(BF16)
        rhs = _dot_nt(at_m, state_b) + _dot(a_ak.astype(BF16), v_m)
        u_b = _dot(inv.astype(BF16), rhs.astype(BF16)).astype(BF16)
        uv = jnp.concatenate([u_b, v_m], axis=0)
        y = _dot_nt(rt_m, state_b) + _dot(jnp.concatenate([a_rb, a_rk], axis=1).astype(BF16), uv)
        y_ref[:, sl] = y[0:C, :] + y[C:2 * C, :]
        s_ref[p] = state * gc_ref[0][:, sl] + _dot_tn(uv, jnp.concatenate([bh_m, kh_m], axis=0))


def _rwkv_scan(rt, at, bt, kt, bh, kh, v, gc, batch, seq):
    T, rdim = rt.shape
    C = SCAN_CHUNK
    nchunk = seq // C
    slab_spec = pl.BlockSpec((C, rdim), lambda b, c: (b * nchunk + c, 0))
    gc3 = gc.reshape(T // C, 1, rdim)
    return pl.pallas_call(
        _scan_kernel,
        grid=(batch, nchunk),
        in_specs=[slab_spec] * 7 + [pl.BlockSpec((1, 1, rdim), lambda b, c: (b * nchunk + c, 0, 0))],
        out_specs=slab_spec,
        out_shape=jax.ShapeDtypeStruct((T, rdim), F32),
        scratch_shapes=[pltpu.VMEM((rdim // LANES, LANES, LANES), F32)],
        compiler_params=_params("arbitrary", "arbitrary"),
        name="rwkv_scan",
    )(rt, at, bt, kt, bh, kh, v, gc3)


def _mla_prep_kernel(x_ref, pos_ref, wm_ref, qg_ref, kvg_ref, wuq_ref, wuk_ref, wuv_ref, freq_ref, sign_ref,
                     q_ref, k_ref, v_ref, *, q_lora, kv_lora, heads):
    z = _dot(x_ref[...].astype(BF16), wm_ref[...])
    c_q = z[:, 0:q_lora]
    c_kv = z[:, q_lora:q_lora + kv_lora]
    kpe = z[:, q_lora + kv_lora:q_lora + kv_lora + LANES]

    def rms(t, gain):
        return t * lax.rsqrt(jnp.mean(t * t, axis=-1, keepdims=True) + RMS_EPS) * gain

    q = _dot(rms(c_q, qg_ref[...]).astype(BF16), wuq_ref[...])
    kvn = rms(c_kv, kvg_ref[...]).astype(BF16)
    kn = _dot(kvn, wuk_ref[...])
    v_ref[...] = _dot(kvn, wuv_ref[...]).astype(BF16)

    ang = pos_ref[...].astype(F32) * freq_ref[...]
    cs = jnp.cos(ang)
    sn = jnp.sin(ang) * sign_ref[...]
    lane = lax.broadcasted_iota(jnp.int32, (1, LANES), 1)
    is_x1 = (lane >= HEAD_DIM) & (lane < HEAD_DIM + HALF_ROPE)

    def rope(t):
        partner = jnp.where(is_x1, pltpu.roll(t, LANES - HALF_ROPE, axis=1), pltpu.roll(t, HALF_ROPE, axis=1))
        return t * cs + partner * sn

    kpe_rot = rope(kpe)
    for h in range(heads):
        sl = slice(h * LANES, (h + 1) * LANES)
        q_ref[:, sl] = rope(q[:, sl]).astype(BF16)
        k_ref[:, sl] = (kn[:, sl] + kpe_rot).astype(BF16)


def _mla_prep(x2, pos2, wm, qg, kvg, wuq, wuk, wuv, freq, sign, heads, tm):
    T, D = x2.shape
    q_lora, kv_lora = qg.shape[-1], kvg.shape[-1]
    row_spec = lambda w: pl.BlockSpec((tm, w), lambda i: (i, 0))
    consts = (wm, qg, kvg, wuq, wuk, wuv, freq, sign)
    return pl.pallas_call(
        functools.partial(_mla_prep_kernel, q_lora=q_lora, kv_lora=kv_lora, heads=heads),
        grid=(T // tm,),
        in_specs=[row_spec(D), row_spec(1)] + [_const_spec(c.shape) for c in consts],
        out_specs=(row_spec(heads * LANES), row_spec(heads * LANES), row_spec(heads * HEAD_DIM)),
        out_shape=(jax.ShapeDtypeStruct((T, heads * LANES), BF16), jax.ShapeDtypeStruct((T, heads * LANES), BF16),
                   jax.ShapeDtypeStruct((T, heads * HEAD_DIM), BF16)),
        compiler_params=_params("arbitrary"),
        name="mla_prep",
    )(x2, pos2, *consts)


def _attn_kernel(q_ref, k_ref, v_ref, o_ref, m_ref, l_ref, acc_ref, *, scale):
    tq = q_ref.shape[0]
    tk = tq
    qi = pl.program_id(2)
    m_ref[...] = jnp.full_like(m_ref, NEG_BIG)
    l_ref[...] = jnp.zeros_like(l_ref)
    acc_ref[...] = jnp.zeros_like(acc_ref)
    causal = lax.broadcasted_iota(jnp.int32, (1, tk), 1) <= lax.broadcasted_iota(jnp.int32, (tq, 1), 0)

    def step(kb, masked):
        start = pl.multiple_of(kb * tk, tk)
        kblk = k_ref[pl.ds(start, tk), :]
        vblk = v_ref[pl.ds(start, tk), :]
        for h in range(2):
            sl = slice(h * LANES, (h + 1) * LANES)
            s = _dot_nt(q_ref[:, sl], kblk[:, sl]) * scale
            if masked:
                s = jnp.where(causal, s, NEG_BIG)
            m_old = m_ref[h]
            m_new = jnp.maximum(m_old, jnp.max(s, axis=1, keepdims=True))
            p = jnp.exp(s - jnp.tile(m_new, (1, tk // LANES)))
            alpha = jnp.exp(m_old - m_new)
            l_ref[h] = alpha * l_ref[h] + jnp.sum(p, axis=1, keepdims=True)
            acc_ref[h] = alpha * acc_ref[h] + _dot(p.astype(BF16), vblk)
            m_ref[h] = m_new

    def body(kb, carry):
        step(kb, False)
        return carry

    lax.fori_loop(0, qi, body, 0)
    step(qi, True)
    lane = lax.broadcasted_iota(jnp.int32, (1, LANES), 1)
    out = jnp.where(lane < HEAD_DIM, acc_ref[0] / l_ref[0], acc_ref[1] / l_ref[1])
    o_ref[...] = out.astype(o_ref.dtype)


def _mla_attn(q, k, v, batch, seq, heads, tq):
    T = q.shape[0]
    nq = seq // tq
    scale = (HEAD_DIM + ROPE_DIM) ** -0.5
    return pl.pallas_call(
        functools.partial(_attn_kernel, scale=scale),
        grid=(batch, heads // 2, nq),
        in_specs=[pl.BlockSpec((tq, 2 * LANES), lambda b, p, i: (b * nq + i, p)),
                  pl.BlockSpec((seq, 2 * LANES), lambda b, p, i: (b, p)),
                  pl.BlockSpec((seq, LANES), lambda b, p, i: (b, p))],
        out_specs=pl.BlockSpec((tq, LANES), lambda b, p, i: (b * nq + i, p)),
        out_shape=jax.ShapeDtypeStruct((T, heads * HEAD_DIM), BF16),
        scratch_shapes=[pltpu.VMEM((2, tq, LANES), F32), pltpu.VMEM((2, tq, LANES), F32),
                        pltpu.VMEM((2, tq, LANES), F32)],
        compiler_params=_params("arbitrary", "arbitrary", "arbitrary"),
        name="mla_attn",
    )(q, k, v)


def _layer_norm(t, gain, bias):
    mu = jnp.mean(t, axis=-1, keepdims=True)
    c = t - mu
    var = jnp.mean(c * c, axis=-1, keepdims=True)
    return c * lax.rsqrt(var + LN_EPS) * gain + bias


def _mix_kernel(x_ref, y_ref, bonus_ref, g_ref, ym_ref, wg_ref, wpr_ref, wpm_ref, wo_ref, lnxg_ref, lnxb_ref,
                ln1g_ref, ln1b_ref, hsum_ref, h_ref, *, alpha):
    x = x_ref[...]
    d = x.shape[1]
    hsum = hsum_ref[...]
    y = y_ref[...]
    inv_n = 1.0 / HEAD_DIM
    yc = y - _sum_right(y, hsum) * inv_n
    var = _sum_right(yc * yc, hsum) * inv_n
    yn = yc * lax.rsqrt(var + GN_EPS) * lnxg_ref[...] + lnxb_ref[...]
    y_r = ((yn + bonus_ref[...]) * g_ref[...]).astype(BF16)
    gates = _dot(x.astype(BF16), wg_ref[...])
    merged = (jax.nn.sigmoid(gates[:, 0:d]) * _dot(y_r, wpr_ref[...])
              + jax.nn.sigmoid(gates[:, d:2 * d]) * _dot(ym_ref[...], wpm_ref[...]))
    pre = alpha * x + _dot(merged.astype(BF16), wo_ref[...])
    h_ref[...] = _layer_norm(pre, ln1g_ref[...], ln1b_ref[...])


def _mix(x2, y, bonus, g, ym, wg, wpr, wpm, wo, lnxg, lnxb, ln1g, ln1b, hsum, alpha, tm):
    T, D = x2.shape
    rdim = y.shape[1]
    row_spec = lambda w: pl.BlockSpec((tm, w), lambda i: (i, 0))
    consts = (wg, wpr, wpm, wo, lnxg, lnxb, ln1g, ln1b, hsum)
    return pl.pallas_call(
        functools.partial(_mix_kernel, alpha=alpha),
        grid=(T // tm,),
        in_specs=[row_spec(D), row_spec(rdim), row_spec(rdim), row_spec(rdim), row_spec(ym.shape[1])]
        + [_const_spec(c.shape) for c in consts],
        out_specs=row_spec(D),
        out_shape=jax.ShapeDtypeStruct((T, D), F32),
        compiler_params=_params("arbitrary"),
        name="mix",
    )(x2, y, bonus, g, ym, *consts)


def _ffn_kernel(h_ref, wg_ref, wu_ref, wd_ref, ln2g_ref, ln2b_ref, o_ref, *, alpha, ff_chunk):
    h = h_ref[...]
    hb = h.astype(BF16)
    acc = alpha * h
    for c in range(wg_ref.shape[1] // ff_chunk):
        sl = slice(c * ff_chunk, (c + 1) * ff_chunk)
        gate = _dot(hb, wg_ref[:, sl])
        act = (gate * jax.nn.sigmoid(gate) * _dot(hb, wu_ref[:, sl])).astype(BF16)
        acc = acc + _dot(act, wd_ref[sl, :])
    o_ref[...] = _layer_norm(acc, ln2g_ref[...], ln2b_ref[...])


def _ffn(h, wg, wu, wd, ln2g, ln2b, alpha, tm):
    T, D = h.shape
    dff = wg.shape[1]
    ff_chunk = dff // 2 if (dff // 2) % LANES == 0 else dff
    row_spec = pl.BlockSpec((tm, D), lambda i: (i, 0))
    consts = (wg, wu, wd, ln2g, ln2b)
    return pl.pallas_call(
        functools.partial(_ffn_kernel, alpha=alpha, ff_chunk=ff_chunk),
        grid=(T // tm,),
        in_specs=[row_spec] + [_const_spec(c.shape) for c in consts],
        out_specs=row_spec,
        out_shape=jax.ShapeDtypeStruct((T, D), F32),
        compiler_params=_params("arbitrary"),
        name="ffn",
    )(h, *consts)


def _layer(x2, pos2, batch, seq, alpha, w_in, mu_shift, w_decay_up, w_decay_base, w_aaa_up, w_aaa_base, w_gate_up,
           k_k, k_a, r_k, lnx_g, lnx_b, q_norm_g, w_uq, kv_norm_g, w_ukv, w_proj_rwkv, w_proj_mla, w_out,
           ln1_g, ln1_b, w_ffn_gate, w_ffn_up, w_ffn_down, ln2_g, ln2_b):
    D = x2.shape[1]
    rdim = k_k.shape[0]
    dl, al, gl = w_decay_up.shape[0], w_aaa_up.shape[0], w_gate_up.shape[0]
    q_lora, kv_lora = q_norm_g.shape[0], kv_norm_g.shape[0]
    heads = w_uq.shape[1] // (HEAD_DIM + ROPE_DIM)
    assert rdim % LANES == 0 and dl + al == LANES and gl == LANES and heads % 2 == 0
    assert w_ukv.shape[1] == heads * 2 * HEAD_DIM and w_proj_mla.shape[0] == heads * HEAD_DIM
    shift_cols = 3 * rdim + dl + al + gl
    tm = min(512, seq)
    assert seq % tm == 0 and tm % SCAN_CHUNK == 0
    row = lambda t: t.reshape(1, -1).astype(F32)

    ws = w_in[:, :shift_cols].astype(BF16)
    wdu = jnp.concatenate([w_decay_up, jnp.zeros((al, rdim), F32)], axis=0).astype(BF16)
    wau = jnp.concatenate([jnp.zeros((dl, rdim), F32), w_aaa_up], axis=0).astype(BF16)
    hid = jnp.arange(rdim) // HEAD_DIM
    hsum = (hid[:, None] == hid[None, :]).astype(BF16)
    rt, at, bt, kt, bh, kh, v, gc, bonus, g = _rwkv_prep(
        x2, ws, row(mu_shift), wdu, wau, w_gate_up.astype(BF16), row(w_decay_base), row(w_aaa_base), row(k_k),
        row(k_a), row(r_k), hsum, seq, tm)
    y = _rwkv_scan(rt, at, bt, kt, bh, kh, v, gc, batch, seq)

    o = shift_cols
    w_cq, w_ckv = w_in[:, o:o + q_lora], w_in[:, o + q_lora:o + q_lora + kv_lora]
    w_kpe = w_in[:, o + q_lora + kv_lora:o + q_lora + kv_lora + ROPE_DIM]
    pad_r = LANES - HEAD_DIM - ROPE_DIM
    wm = jnp.concatenate([w_cq, w_ckv, jnp.zeros((D, HEAD_DIM), F32), w_kpe, jnp.zeros((D, pad_r), F32)],
                         axis=1).astype(BF16)
    wuq = jnp.pad(w_uq.reshape(q_lora, heads, HEAD_DIM + ROPE_DIM), ((0, 0), (0, 0), (0, pad_r)))
    wuq = wuq.reshape(q_lora, heads * LANES).astype(BF16)
    ukv = w_ukv.reshape(kv_lora, heads, 2 * HEAD_DIM)
    wuk = jnp.pad(ukv[:, :, :HEAD_DIM], ((0, 0), (0, 0), (0, LANES - HEAD_DIM))).reshape(kv_lora, heads * LANES)
    wuv = ukv[:, :, HEAD_DIM:].reshape(kv_lora, heads * HEAD_DIM)
    inv_freq = ROPE_THETA ** (-jnp.arange(0, ROPE_DIM, 2, dtype=F32) / ROPE_DIM)
    zeros = lambda n: jnp.zeros((n,), F32)
    freq = jnp.concatenate([zeros(HEAD_DIM), inv_freq, inv_freq, zeros(pad_r)]).reshape(1, LANES)
    sign = jnp.concatenate([zeros(HEAD_DIM), -jnp.ones((HALF_ROPE,), F32), jnp.ones((HALF_ROPE,), F32),
                            zeros(pad_r)]).reshape(1, LANES)
    q, k, vv = _mla_prep(x2, pos2, wm, row(q_norm_g), row(kv_norm_g), wuq, wuk.astype(BF16), wuv.astype(BF16),
                         freq, sign, heads, tm)
    ym = _mla_attn(q, k, vv, batch, seq, heads, tm)

    o2 = o + q_lora + kv_lora + ROPE_DIM
    wg = w_in[:, o2:o2 + 2 * D].astype(BF16)
    h = _mix(x2, y, bonus, g, ym, wg, w_proj_rwkv.astype(BF16), w_proj_mla.astype(BF16), w_out.astype(BF16),
             row(lnx_g), row(lnx_b), row(ln1_g), row(ln1_b), hsum, alpha, tm)
    return _ffn(h, w_ffn_gate.astype(BF16), w_ffn_up.astype(BF16), w_ffn_down.astype(BF16), row(ln2_g), row(ln2_b),
                alpha, tm)


def kernel(x, positions, w_in, mu_shift, w_decay_up, w_decay_base, w_aaa_up, w_aaa_base, w_gate_up, k_k, k_a, r_k,
           lnx_g, lnx_b, q_norm_g, w_uq, kv_norm_g, w_ukv, w_proj_rwkv, w_proj_mla, w_out, ln1_g, ln1_b,
           w_ffn_gate, w_ffn_up, w_ffn_down, ln2_g, ln2_b):
    batch, seq, D = x.shape
    depth = w_in.shape[0]
    alpha = (2.0 * depth) ** 0.25
    h = x.reshape(batch * seq, D)
    pos2 = positions.reshape(batch * seq, 1)
    per_layer = (w_in, mu_shift, w_decay_up, w_decay_base, w_aaa_up, w_aaa_base, w_gate_up, k_k, k_a, r_k, lnx_g,
                 lnx_b, q_norm_g, w_uq, kv_norm_g, w_ukv, w_proj_rwkv, w_proj_mla, w_out, ln1_g, ln1_b, w_ffn_gate,
                 w_ffn_up, w_ffn_down, ln2_g, ln2_b)
    for l in range(depth):
        h = _layer(h, pos2, batch, seq, alpha, *(p[l] for p in per_layer))
    return h.reshape(batch, seq, D)
```

```python
import functools
import math

import jax
import jax.numpy as jnp
from jax import lax
from jax.experimental import pallas as pl
from jax.experimental.pallas import tpu as pltpu

F32 = jnp.float32
BF16 = jnp.bfloat16

HEAD_DIM = 64
ROPE_DIM = 32
HALF_ROPE = ROPE_DIM // 2
ROPE_THETA = 10000.0
GN_EPS = 64e-5
LN_EPS = 1e-5
RMS_EPS = 1e-6
LANES = 128
SCAN_CHUNK = 64
VMEM_LIMIT = 56 * 1024 * 1024
NEG_BIG = -1e30


def _dot(a, b):
    return jnp.dot(a, b, preferred_element_type=F32)


def _dot_nt(a, b):
    return lax.dot_general(a, b, (((1,), (1,)), ((), ())), preferred_element_type=F32)


def _dot_tn(a, b):
    return lax.dot_general(a, b, (((0,), (0,)), ((), ())), preferred_element_type=F32)


def _split3(x):
    hi = x.astype(BF16)
    r1 = x - hi.astype(F32)
    mid = r1.astype(BF16)
    lo = (r1 - mid.astype(F32)).astype(BF16)
    return hi, mid, lo


def _sum_right(x, m):
    hi = x.astype(BF16)
    lo = (x - hi.astype(F32)).astype(BF16)
    return _dot(hi, m) + _dot(lo, m)


def _sum_left(m, x):
    hi, mid, lo = _split3(x)
    return _dot(m, hi) + _dot(m, mid) + _dot(m, lo)


def _const_spec(shape):
    nd = len(shape)
    return pl.BlockSpec(shape, lambda *_: (0,) * nd, pipeline_mode=pl.Buffered(1))


def _params(*sem):
    return pltpu.CompilerParams(dimension_semantics=sem, vmem_limit_bytes=VMEM_LIMIT)


def _rwkv_prep_kernel(x_ref, ws_ref, mu_ref, wdu_ref, wau_ref, wgu_ref, dbase_ref, abase_ref, kk_ref, ka_ref,
                      rk_ref, hsum_ref, ctri_ref, call_ref, csel_ref,
                      rt_ref, at_ref, bt_ref, kt_ref, bh_ref, kh_ref, v_ref, gc_ref, bonus_ref, g_ref,
                      carry_ref, *, tiles_per_seq, rdim):
    tm = x_ref.shape[0]
    first = (pl.program_id(0) % tiles_per_seq) == 0

    @pl.when(first)
    def _():
        carry_ref[...] = jnp.zeros_like(carry_ref)

    z = _dot(x_ref[...].astype(BF16), ws_ref[...])
    prev_last = carry_ref[0:1, :]
    carry_ref[0:1, :] = z[tm - 1:tm, :]
    row = lax.broadcasted_iota(jnp.int32, (tm, 1), 0)
    z_prev = jnp.where(row == 0, prev_last, pltpu.roll(z, 1, axis=0))
    zs = z + mu_ref[...] * (z_prev - z)

    z_r = zs[:, 0:rdim]
    z_k = zs[:, rdim:2 * rdim]
    z_v = zs[:, 2 * rdim:3 * rdim]
    z_lora = zs[:, 3 * rdim:3 * rdim + LANES]
    z_gd = zs[:, 3 * rdim + LANES:3 * rdim + 2 * LANES]

    t = -(dbase_ref[...] + _dot(jnp.tanh(z_lora).astype(BF16), wdu_ref[...]))
    softplus = jnp.maximum(t, 0.0) + jnp.log1p(jnp.exp(-jnp.abs(t)))
    logw = -jnp.exp(-softplus - 0.5)
    a = jax.nn.sigmoid(abase_ref[...] + _dot(z_lora.astype(BF16), wau_ref[...]))
    g_ref[...] = _dot(jax.nn.sigmoid(z_gd).astype(BF16), wgu_ref[...])

    hsum = hsum_ref[...]
    kk = z_k * kk_ref[...]
    kk = kk / jnp.maximum(jnp.sqrt(_sum_right(kk * kk, hsum)), 1e-12)
    kmod = z_k * (1.0 + (a - 1.0) * ka_ref[...])
    b = kk * a
    bonus_ref[...] = _sum_right(z_r * kmod * rk_ref[...], hsum) * z_v

    cum = _sum_left(ctri_ref[...], logw)
    tot = _sum_left(call_ref[...], logw)
    gc_ref[...] = jnp.exp(_sum_left(csel_ref[...], logw)[0:gc_ref.shape[0], :])

    e_neg = jnp.exp(-cum)
    e_tail = jnp.exp(tot - cum)
    rt_ref[...] = (z_r * jnp.exp(cum)).astype(BF16)
    at_ref[...] = (-kk * jnp.exp(cum - logw)).astype(BF16)
    bt_ref[...] = (b * e_neg).astype(BF16)
    kt_ref[...] = (kmod * e_neg).astype(BF16)
    bh_ref[...] = (b * e_tail).astype(BF16)
    kh_ref[...] = (kmod * e_tail).astype(BF16)
    v_ref[...] = z_v.astype(BF16)


def _rwkv_prep(x2, ws, mu, wdu, wau, wgu, dbase, abase, k_k, k_a, r_k, hsum, seq, tm):
    T, D = x2.shape
    rdim = k_k.shape[-1]
    ncol = ws.shape[1]
    cpt = tm // SCAN_CHUNK
    r = jnp.arange(tm)
    same = (r[:, None] // SCAN_CHUNK) == (r[None, :] // SCAN_CHUNK)
    ctri = (same & (r[None, :] <= r[:, None])).astype(BF16)
    call = same.astype(BF16)
    csel_rows = max(cpt, 16)
    csel = ((jnp.arange(csel_rows)[:, None] == (r[None, :] // SCAN_CHUNK))).astype(BF16)

    row_spec = lambda w: pl.BlockSpec((tm, w), lambda i: (i, 0))
    slab = jax.ShapeDtypeStruct((T, rdim), BF16)
    slab32 = jax.ShapeDtypeStruct((T, rdim), F32)
    out_shape = (slab,) * 7 + (jax.ShapeDtypeStruct((T // SCAN_CHUNK, rdim), F32), slab32, slab32)
    out_specs = (row_spec(rdim),) * 7 + (pl.BlockSpec((cpt, rdim), lambda i: (i, 0)), row_spec(rdim), row_spec(rdim))
    consts = (ws, mu, wdu, wau, wgu, dbase, abase, k_k, k_a, r_k, hsum, ctri, call, csel)
    return pl.pallas_call(
        functools.partial(_rwkv_prep_kernel, tiles_per_seq=seq // tm, rdim=rdim),
        grid=(T // tm,),
        in_specs=[row_spec(D)] + [_const_spec(c.shape) for c in consts],
        out_specs=out_specs,
        out_shape=out_shape,
        scratch_shapes=[pltpu.VMEM((8, ncol), F32)],
        compiler_params=_params("arbitrary"),
        name="rwkv_prep",
    )(x2, *consts)


def _scan_kernel(rt_ref, at_ref, bt_ref, kt_ref, bh_ref, kh_ref, v_ref, gc_ref, y_ref, s_ref):
    C = SCAN_CHUNK
    nb = rt_ref.shape[0]
    npairs = rt_ref.shape[2] // LANES
    chains = [(g, p) for g in range(nb) for p in range(npairs)]

    @pl.when(pl.program_id(1) == 0)
    def _():
        s_ref[...] = jnp.zeros_like(s_ref)

    row = lax.broadcasted_iota(jnp.int32, (2 * C, 1), 0)
    lane = lax.broadcasted_iota(jnp.int32, (1, LANES), 1)
    keep = (row < C) == (lane < HEAD_DIM)
    t_idx = row % C
    s_idx = lane % C
    m_strict = keep & (s_idx < t_idx)
    m_incl = keep & (s_idx <= t_idx)
    eye = jnp.where(keep & (s_idx == t_idx), 1.0, 0.0)
    keep_b = jnp.where(keep, 1.0, 0.0).astype(BF16)

    def stacked(ref, g, p):
        x = ref[g, :, p * LANES:(p + 1) * LANES]
        return jnp.concatenate([x, x], axis=0) * keep_b

    gram = [_dot_nt(jnp.concatenate([stacked(at_ref, g, p), stacked(rt_ref, g, p)], axis=0),
                    jnp.concatenate([stacked(bt_ref, g, p), stacked(kt_ref, g, p)], axis=0)) for g, p in chains]
    a_ab = [jnp.where(m_strict, gm[0:2 * C, 0:LANES], 0.0) for gm in gram]
    a_ak = [jnp.where(m_strict, gm[0:2 * C, LANES:2 * LANES], 0.0).astype(BF16) for gm in gram]
    a_r = [jnp.concatenate([jnp.where(m_incl, gm[2 * C:4 * C, 0:LANES], 0.0),
                            jnp.where(m_incl, gm[2 * C:4 * C, LANES:2 * LANES], 0.0)], axis=1).astype(BF16)
           for gm in gram]

    inv = [eye + a for a in a_ab]
    pw = [a.astype(BF16) for a in a_ab]
    for _ in range(int(math.log2(C)) - 1):
        pw = [_dot(q, q).astype(BF16) for q in pw]
        inv = [x + _dot(x.astype(BF16), q) for x, q in zip(inv, pw)]

    state = [s_ref[i] for i in range(len(chains))]
    state_b = [s.astype(BF16) for s in state]
    rhs = [_dot_nt(stacked(at_ref, g, p), sb) + _dot(ak, stacked(v_ref, g, p))
           for (g, p), sb, ak in zip(chains, state_b, a_ak)]
    u_b = [_dot(x.astype(BF16), r.astype(BF16)).astype(BF16) for x, r in zip(inv, rhs)]
    uv = [jnp.concatenate([u, stacked(v_ref, g, p)], axis=0) for (g, p), u in zip(chains, u_b)]
    y = [_dot_nt(stacked(rt_ref, g, p), sb) + _dot(ar, w) for (g, p), sb, ar, w in zip(chains, state_b, a_r, uv)]
    new_state = [_dot_tn(w, jnp.concatenate([stacked(bh_ref, g, p), stacked(kh_ref, g, p)], axis=0))
                 for (g, p), w in zip(chains, uv)]
    for i, (g, p) in enumerate(chains):
        sl = slice(p * LANES, (p + 1) * LANES)
        y_ref[g, :, sl] = y[i][0:C, :] + y[i][C:2 * C, :]
        s_ref[i] = state[i] * gc_ref[g, 0, :, sl] + new_state[i]


def _rwkv_scan(rt, at, bt, kt, bh, kh, v, gc, batch, seq, nb):
    T, rdim = rt.shape
    C = SCAN_CHUNK
    nchunk = seq // C
    view = lambda t: t.reshape(batch, seq, rdim)
    slab_spec = pl.BlockSpec((nb, C, rdim), lambda b, c: (b, c, 0))
    y = pl.pallas_call(
        _scan_kernel,
        grid=(batch // nb, nchunk),
        in_specs=[slab_spec] * 7 + [pl.BlockSpec((nb, 1, 1, rdim), lambda b, c: (b, c, 0, 0))],
        out_specs=slab_spec,
        out_shape=jax.ShapeDtypeStruct((batch, seq, rdim), F32),
        scratch_shapes=[pltpu.VMEM((nb * rdim // LANES, LANES, LANES), F32)],
        compiler_params=_params("arbitrary", "arbitrary"),
        name="rwkv_scan",
    )(*(view(t) for t in (rt, at, bt, kt, bh, kh, v)), gc.reshape(batch, nchunk, 1, rdim))
    return y.reshape(T, rdim)


def _mla_prep_kernel(x_ref, pos_ref, wm_ref, qg_ref, kvg_ref, wuq_ref, wuk_ref, wuv_ref, freq_ref, sign_ref,
                     q_ref, k_ref, v_ref, *, q_lora, kv_lora, heads):
    z = _dot(x_ref[...].astype(BF16), wm_ref[...])
    c_q = z[:, 0:q_lora]
    c_kv = z[:, q_lora:q_lora + kv_lora]
    kpe = z[:, q_lora + kv_lora:q_lora + kv_lora + LANES]

    def rms(t, gain):
        return t * lax.rsqrt(jnp.mean(t * t, axis=-1, keepdims=True) + RMS_EPS) * gain

    q = _dot(rms(c_q, qg_ref[...]).astype(BF16), wuq_ref[...])
    kvn = rms(c_kv, kvg_ref[...]).astype(BF16)
    kn = _dot(kvn, wuk_ref[...])
    v_ref[...] = _dot(kvn, wuv_ref[...]).astype(BF16)

    ang = pos_ref[...].astype(F32) * freq_ref[...]
    cs = jnp.cos(ang)
    sn = jnp.sin(ang) * sign_ref[...]
    lane = lax.broadcasted_iota(jnp.int32, (1, LANES), 1)
    is_x1 = (lane >= HEAD_DIM) & (lane < HEAD_DIM + HALF_ROPE)

    def rope(t):
        partner = jnp.where(is_x1, pltpu.roll(t, LANES - HALF_ROPE, axis=1), pltpu.roll(t, HALF_ROPE, axis=1))
        return t * cs + partner * sn

    kpe_rot = rope(kpe)
    q_scale = (HEAD_DIM + ROPE_DIM) ** -0.5 * math.log2(math.e)
    for h in range(heads):
        sl = slice(h * LANES, (h + 1) * LANES)
        q_ref[:, sl] = (rope(q[:, sl]) * q_scale).astype(BF16)
        k_ref[:, sl] = (kn[:, sl] + kpe_rot).astype(BF16)


def _mla_prep(x2, pos2, wm, qg, kvg, wuq, wuk, wuv, freq, sign, heads, tm):
    T, D = x2.shape
    q_lora, kv_lora = qg.shape[-1], kvg.shape[-1]
    row_spec = lambda w: pl.BlockSpec((tm, w), lambda i: (i, 0))
    consts = (wm, qg, kvg, wuq, wuk, wuv, freq, sign)
    return pl.pallas_call(
        functools.partial(_mla_prep_kernel, q_lora=q_lora, kv_lora=kv_lora, heads=heads),
        grid=(T // tm,),
        in_specs=[row_spec(D), row_spec(1)] + [_const_spec(c.shape) for c in consts],
        out_specs=(row_spec(heads * LANES), row_spec(heads * LANES), row_spec(heads * HEAD_DIM)),
        out_shape=(jax.ShapeDtypeStruct((T, heads * LANES), BF16), jax.ShapeDtypeStruct((T, heads * LANES), BF16),
                   jax.ShapeDtypeStruct((T, heads * HEAD_DIM), BF16)),
        compiler_params=_params("arbitrary"),
        name="mla_prep",
    )(x2, pos2, *consts)


def _attn_kernel(q_ref, k_ref, v_ref, o_ref, m_ref, l_ref, acc_ref, sa_ref, sb_ref):
    tq = q_ref.shape[0]
    tk = tq
    qi = pl.program_id(2)
    m_ref[...] = jnp.full_like(m_ref, NEG_BIG)
    l_ref[...] = jnp.zeros_like(l_ref)
    acc_ref[...] = jnp.zeros_like(acc_ref)
    causal = lax.broadcasted_iota(jnp.int32, (1, tk), 1) <= lax.broadcasted_iota(jnp.int32, (tq, 1), 0)

    def scores(kb, dst_ref):
        kblk = k_ref[pl.ds(pl.multiple_of(kb * tk, tk), tk), :]
        for h in range(2):
            sl = slice(h * LANES, (h + 1) * LANES)
            dst_ref[h] = _dot_nt(q_ref[:, sl], kblk[:, sl])

    def consume(kb, src_ref, masked):
        vblk = v_ref[pl.ds(pl.multiple_of(kb * tk, tk), tk), :]
        for h in range(2):
            s = src_ref[h]
            if masked:
                s = jnp.where(causal, s, NEG_BIG)
            m_old = m_ref[h]
            m_new = jnp.maximum(m_old, jnp.max(s, axis=1, keepdims=True))
            p = jnp.exp2(s - jnp.tile(m_new, (1, tk // LANES)))
            alpha = jnp.exp2(m_old - m_new)
            l_ref[h] = alpha * l_ref[h] + jnp.sum(p, axis=1, keepdims=True)
            acc_ref[h] = alpha * acc_ref[h] + _dot(p.astype(BF16), vblk)
            m_ref[h] = m_new

    scores(0, sa_ref)

    def body(j, carry):
        scores(2 * j + 1, sb_ref)
        consume(2 * j, sa_ref, False)
        scores(2 * j + 2, sa_ref)
        consume(2 * j + 1, sb_ref, False)
        return carry

    lax.fori_loop(0, qi // 2, body, 0)

    @pl.when(qi % 2 == 0)
    def _():
        consume(qi, sa_ref, True)

    @pl.when(qi % 2 == 1)
    def _():
        scores(qi, sb_ref)
        consume(qi - 1, sa_ref, False)
        consume(qi, sb_ref, True)

    lane = lax.broadcasted_iota(jnp.int32, (1, LANES), 1)
    out = jnp.where(lane < HEAD_DIM, acc_ref[0] / l_ref[0], acc_ref[1] / l_ref[1])
    o_ref[...] = out.astype(o_ref.dtype)


def _mla_attn(q, k, v, batch, seq, heads, tq):
    T = q.shape[0]
    nq = seq // tq
    return pl.pallas_call(
        _attn_kernel,
        grid=(batch, heads // 2, nq),
        in_specs=[pl.BlockSpec((tq, 2 * LANES), lambda b, p, i: (b * nq + i, p)),
                  pl.BlockSpec((seq, 2 * LANES), lambda b, p, i: (b, p)),
                  pl.BlockSpec((seq, LANES), lambda b, p, i: (b, p))],
        out_specs=pl.BlockSpec((tq, LANES), lambda b, p, i: (b * nq + i, p)),
        out_shape=jax.ShapeDtypeStruct((T, heads * HEAD_DIM), BF16),
        scratch_shapes=[pltpu.VMEM((2, tq, LANES), F32), pltpu.VMEM((2, tq, LANES), F32),
                        pltpu.VMEM((2, tq, LANES), F32), pltpu.VMEM((2, tq, tq), F32),
                        pltpu.VMEM((2, tq, tq), F32)],
        compiler_params=_params("arbitrary", "arbitrary", "arbitrary"),
        name="mla_attn",
    )(q, k, v)


def _layer_norm(t, gain, bias):
    mu = jnp.mean(t, axis=-1, keepdims=True)
    c = t - mu
    var = jnp.mean(c * c, axis=-1, keepdims=True)
    return c * lax.rsqrt(var + LN_EPS) * gain + bias


def _mix_kernel(x_ref, y_ref, bonus_ref, g_ref, ym_ref, wg_ref, wpr_ref, wpm_ref, wo_ref, lnxg_ref, lnxb_ref,
                ln1g_ref, ln1b_ref, hsum_ref, h_ref, *, alpha):
    x = x_ref[...]
    d = x.shape[1]
    hsum = hsum_ref[...]
    y = y_ref[...]
    inv_n = 1.0 / HEAD_DIM
    yc = y - _sum_right(y, hsum) * inv_n
    var = _sum_right(yc * yc, hsum) * inv_n
    yn = yc * lax.rsqrt(var + GN_EPS) * lnxg_ref[...] + lnxb_ref[...]
    y_r = ((yn + bonus_ref[...]) * g_ref[...]).astype(BF16)
    gates = _dot(x.astype(BF16), wg_ref[...])
    merged = (jax.nn.sigmoid(gates[:, 0:d]) * _dot(y_r, wpr_ref[...])
              + jax.nn.sigmoid(gates[:, d:2 * d]) * _dot(ym_ref[...], wpm_ref[...]))
    pre = alpha * x + _dot(merged.astype(BF16), wo_ref[...])
    h_ref[...] = _layer_norm(pre, ln1g_ref[...], ln1b_ref[...])


def _mix(x2, y, bonus, g, ym, wg, wpr, wpm, wo, lnxg, lnxb, ln1g, ln1b, hsum, alpha, tm):
    T, D = x2.shape
    rdim = y.shape[1]
    row_spec = lambda w: pl.BlockSpec((tm, w), lambda i: (i, 0))
    consts = (wg, wpr, wpm, wo, lnxg, lnxb, ln1g, ln1b, hsum)
    return pl.pallas_call(
        functools.partial(_mix_kernel, alpha=alpha),
        grid=(T // tm,),
        in_specs=[row_spec(D), row_spec(rdim), row_spec(rdim), row_spec(rdim), row_spec(ym.shape[1])]
        + [_const_spec(c.shape) for c in consts],
        out_specs=row_spec(D),
        out_shape=jax.ShapeDtypeStruct((T, D), F32),
        compiler_params=_params("arbitrary"),
        name="mix",
    )(x2, y, bonus, g, ym, *consts)


def _ffn_kernel(h_ref, wg_ref, wu_ref, wd_ref, ln2g_ref, ln2b_ref, o_ref, *, alpha, ff_chunk):
    h = h_ref[...]
    hb = h.astype(BF16)
    acc = alpha * h
    for c in range(wg_ref.shape[1] // ff_chunk):
        sl = slice(c * ff_chunk, (c + 1) * ff_chunk)
        gate = _dot(hb, wg_ref[:, sl])
        act = (gate * jax.nn.sigmoid(gate) * _dot(hb, wu_ref[:, sl])).astype(BF16)
        acc = acc + _dot(act, wd_ref[sl, :])
    o_ref[...] = _layer_norm(acc, ln2g_ref[...], ln2b_ref[...])


def _ffn(h, wg, wu, wd, ln2g, ln2b, alpha, tm):
    T, D = h.shape
    dff = wg.shape[1]
    ff_chunk = dff // 2 if (dff // 2) % LANES == 0 else dff
    row_spec = pl.BlockSpec((tm, D), lambda i: (i, 0))
    consts = (wg, wu, wd, ln2g, ln2b)
    return pl.pallas_call(
        functools.partial(_ffn_kernel, alpha=alpha, ff_chunk=ff_chunk),
        grid=(T // tm,),
        in_specs=[row_spec] + [_const_spec(c.shape) for c in consts],
        out_specs=row_spec,
        out_shape=jax.ShapeDtypeStruct((T, D), F32),
        compiler_params=_params("arbitrary"),
        name="ffn",
    )(h, *consts)


def _layer(x2, pos2, batch, seq, alpha, w_in, mu_shift, w_decay_up, w_decay_base, w_aaa_up, w_aaa_base, w_gate_up,
           k_k, k_a, r_k, lnx_g, lnx_b, q_norm_g, w_uq, kv_norm_g, w_ukv, w_proj_rwkv, w_proj_mla, w_out,
           ln1_g, ln1_b, w_ffn_gate, w_ffn_up, w_ffn_down, ln2_g, ln2_b):
    D = x2.shape[1]
    rdim = k_k.shape[0]
    dl, al, gl = w_decay_up.shape[0], w_aaa_up.shape[0], w_gate_up.shape[0]
    q_lora, kv_lora = q_norm_g.shape[0], kv_norm_g.shape[0]
    heads = w_uq.shape[1] // (HEAD_DIM + ROPE_DIM)
    assert rdim % LANES == 0 and dl + al == LANES and gl == LANES and heads % 2 == 0
    assert w_ukv.shape[1] == heads * 2 * HEAD_DIM and w_proj_mla.shape[0] == heads * HEAD_DIM
    shift_cols = 3 * rdim + dl + al + gl
    tm = min(512, seq)
    assert seq % tm == 0 and tm % SCAN_CHUNK == 0
    row = lambda t: t.reshape(1, -1).astype(F32)

    ws = w_in[:, :shift_cols].astype(BF16)
    wdu = jnp.concatenate([w_decay_up, jnp.zeros((al, rdim), F32)], axis=0).astype(BF16)
    wau = jnp.concatenate([jnp.zeros((dl, rdim), F32), w_aaa_up], axis=0).astype(BF16)
    hid = jnp.arange(rdim) // HEAD_DIM
    hsum = (hid[:, None] == hid[None, :]).astype(BF16)
    rt, at, bt, kt, bh, kh, v, gc, bonus, g = _rwkv_prep(
        x2, ws, row(mu_shift), wdu, wau, w_gate_up.astype(BF16), row(w_decay_base), row(w_aaa_base), row(k_k),
        row(k_a), row(r_k), hsum, seq, tm)
    y = _rwkv_scan(rt, at, bt, kt, bh, kh, v, gc, batch, seq, 4 if batch % 4 == 0 else 1)

    o = shift_cols
    w_cq, w_ckv = w_in[:, o:o + q_lora], w_in[:, o + q_lora:o + q_lora + kv_lora]
    w_kpe = w_in[:, o + q_lora + kv_lora:o + q_lora + kv_lora + ROPE_DIM]
    pad_r = LANES - HEAD_DIM - ROPE_DIM
    wm = jnp.concatenate([w_cq, w_ckv, jnp.zeros((D, HEAD_DIM), F32), w_kpe, jnp.zeros((D, pad_r), F32)],
                         axis=1).astype(BF16)
    wuq = jnp.pad(w_uq.reshape(q_lora, heads, HEAD_DIM + ROPE_DIM), ((0, 0), (0, 0), (0, pad_r)))
    wuq = wuq.reshape(q_lora, heads * LANES).astype(BF16)
    ukv = w_ukv.reshape(kv_lora, heads, 2 * HEAD_DIM)
    wuk = jnp.pad(ukv[:, :, :HEAD_DIM], ((0, 0), (0, 0), (0, LANES - HEAD_DIM))).reshape(kv_lora, heads * LANES)
    wuv = ukv[:, :, HEAD_DIM:].reshape(kv_lora, heads * HEAD_DIM)
    inv_freq = ROPE_THETA ** (-jnp.arange(0, ROPE_DIM, 2, dtype=F32) / ROPE_DIM)
    zeros = lambda n: jnp.zeros((n,), F32)
    freq = jnp.concatenate([zeros(HEAD_DIM), inv_freq, inv_freq, zeros(pad_r)]).reshape(1, LANES)
    sign = jnp.concatenate([zeros(HEAD_DIM), -jnp.ones((HALF_ROPE,), F32), jnp.ones((HALF_ROPE,), F32),
                            zeros(pad_r)]).reshape(1, LANES)
    q, k, vv = _mla_prep(x2, pos2, wm, row(q_norm_g), row(kv_norm_g), wuq, wuk.astype(BF16), wuv.astype(BF16),
                         freq, sign, heads, tm)
    ym = _mla_attn(q, k, vv, batch, seq, heads, tm)

    o2 = o + q_lora + kv_lora + ROPE_DIM
    wg = w_in[:, o2:o2 + 2 * D].astype(BF16)
    h = _mix(x2, y, bonus, g, ym, wg, w_proj_rwkv.astype(BF16), w_proj_mla.astype(BF16), w_out.astype(BF16),
             row(lnx_g), row(lnx_b), row(ln1_g), row(ln1_b), hsum, alpha, tm)
    return _ffn(h, w_ffn_gate.astype(BF16), w_ffn_up.astype(BF16), w_ffn_down.astype(BF16), row(ln2_g), row(ln2_b),
                alpha, tm)


def kernel(x, positions, w_in, mu_shift, w_decay_up, w_decay_base, w_aaa_up, w_aaa_base, w_gate_up, k_k, k_a, r_k,
           lnx_g, lnx_b, q_norm_g, w_uq, kv_norm_g, w_ukv, w_proj_rwkv, w_proj_mla, w_out, ln1_g, ln1_b,
           w_ffn_gate, w_ffn_up, w_ffn_down, ln2_g, ln2_b):
    batch, seq, D = x.shape
    depth = w_in.shape[0]
    alpha = (2.0 * depth) ** 0.25
    h = x.reshape(batch * seq, D)
    pos2 = positions.reshape(batch * seq, 1)
    per_layer = (w_in, mu_shift, w_decay_up, w_decay_base, w_aaa_up, w_aaa_base, w_gate_up, k_k, k_a, r_k, lnx_g,
                 lnx_b, q_norm_g, w_uq, kv_norm_g, w_ukv, w_proj_rwkv, w_proj_mla, w_out, ln1_g, ln1_b, w_ffn_gate,
                 w_ffn_up, w_ffn_down, ln2_g, ln2_b)
    for l in range(depth):
        h = _layer(h, pos2, batch, seq, alpha, *(p[l] for p in per_layer))
    return h.reshape(batch, seq, D)
```

```python
import functools
import math

import jax
import jax.numpy as jnp
from jax import lax
from jax.experimental import pallas as pl
from jax.experimental.pallas import tpu as pltpu

F32 = jnp.float32
BF16 = jnp.bfloat16

HEAD_DIM = 64
ROPE_DIM = 32
HALF_ROPE = ROPE_DIM // 2
ROPE_THETA = 10000.0
GN_EPS = 64e-5
LN_EPS = 1e-5
RMS_EPS = 1e-6
LANES = 128
MXU_TILE = 256
SCAN_CHUNK = 64
VMEM_LIMIT = 56 * 1024 * 1024
NEG_BIG = -1e30


def _dot(a, b):
    return jnp.dot(a, b, preferred_element_type=F32)


def _dot_nt(a, b):
    return lax.dot_general(a, b, (((1,), (1,)), ((), ())), preferred_element_type=F32)


def _dot_tn(a, b):
    return lax.dot_general(a, b, (((0,), (0,)), ((), ())), preferred_element_type=F32)


def _split3(x):
    hi = x.astype(BF16)
    r1 = x - hi.astype(F32)
    mid = r1.astype(BF16)
    lo = (r1 - mid.astype(F32)).astype(BF16)
    return hi, mid, lo


def _sum_right(x, m):
    hi = x.astype(BF16)
    lo = (x - hi.astype(F32)).astype(BF16)
    return _dot(hi, m) + _dot(lo, m)


def _sum_left(m, x):
    hi, mid, lo = _split3(x)
    return _dot(m, hi) + _dot(m, mid) + _dot(m, lo)


def _const_spec(shape):
    nd = len(shape)
    return pl.BlockSpec(shape, lambda *_: (0,) * nd, pipeline_mode=pl.Buffered(1))


def _params(*sem):
    return pltpu.CompilerParams(dimension_semantics=sem, vmem_limit_bytes=VMEM_LIMIT)


def _rwkv_prep_kernel(x_ref, ws_ref, mu_ref, wdu_ref, wau_ref, wgu_ref, dbase_ref, abase_ref, kk_ref, ka_ref,
                      rk_ref, hsum_ref, ctri_ref,
                      rt_ref, at_ref, bt_ref, kt_ref, bh_ref, kh_ref, v_ref, gc_ref, bonus_ref, g_ref,
                      carry_ref, *, tiles_per_seq, rdim):
    tm = x_ref.shape[0]
    first = (pl.program_id(0) % tiles_per_seq) == 0

    @pl.when(first)
    def _():
        carry_ref[...] = jnp.zeros_like(carry_ref)

    z = _dot(x_ref[...].astype(BF16), ws_ref[...])
    prev_last = carry_ref[0:1, :]
    carry_ref[0:1, :] = z[tm - 1:tm, :]
    row = lax.broadcasted_iota(jnp.int32, (tm, 1), 0)
    z_prev = jnp.where(row == 0, prev_last, pltpu.roll(z, 1, axis=0))
    zs = z + mu_ref[...] * (z_prev - z)

    z_r = zs[:, 0:rdim]
    z_k = zs[:, rdim:2 * rdim]
    z_v = zs[:, 2 * rdim:3 * rdim]
    z_lora = zs[:, 3 * rdim:3 * rdim + LANES]
    z_gd = zs[:, 3 * rdim + LANES:3 * rdim + 2 * LANES]

    t = -(dbase_ref[...] + _dot(jnp.tanh(z_lora).astype(BF16), wdu_ref[...]))
    softplus = jnp.maximum(t, 0.0) + jnp.log1p(jnp.exp(-jnp.abs(t)))
    logw = -jnp.exp(-softplus - 0.5)
    a = jax.nn.sigmoid(abase_ref[...] + _dot(z_lora.astype(BF16), wau_ref[...]))
    g_ref[...] = _dot(jax.nn.sigmoid(z_gd).astype(BF16), wgu_ref[...])

    hsum = hsum_ref[...]
    kk = z_k * kk_ref[...]
    kk = kk / jnp.maximum(jnp.sqrt(_sum_right(kk * kk, hsum)), 1e-12)
    kmod = z_k * (1.0 + (a - 1.0) * ka_ref[...])
    b = kk * a
    bonus_ref[...] = _sum_right(z_r * kmod * rk_ref[...], hsum) * z_v

    v_ref[...] = z_v.astype(BF16)
    ctri = ctri_ref[...]
    C = ctri.shape[0]
    for c in range(tm // C):
        sl = slice(c * C, (c + 1) * C)
        lw = logw[sl, :]
        cum = _sum_left(ctri, lw)
        e_neg = jnp.exp(-cum)
        chunk_decay = jnp.exp(cum[C - 1:C, :])
        gc_ref[c:c + 1, :] = chunk_decay
        b_neg = b[sl, :] * e_neg
        k_neg = kmod[sl, :] * e_neg
        rt_ref[sl, :] = (z_r[sl, :] * jnp.exp(cum)).astype(BF16)
        at_ref[sl, :] = (-kk[sl, :] * jnp.exp(cum - lw)).astype(BF16)
        bt_ref[sl, :] = b_neg.astype(BF16)
        kt_ref[sl, :] = k_neg.astype(BF16)
        bh_ref[sl, :] = (b_neg * chunk_decay).astype(BF16)
        kh_ref[sl, :] = (k_neg * chunk_decay).astype(BF16)


def _rwkv_prep(x2, ws, mu, wdu, wau, wgu, dbase, abase, k_k, k_a, r_k, hsum, seq, tm):
    T, D = x2.shape
    rdim = k_k.shape[-1]
    ncol = ws.shape[1]
    cpt = tm // SCAN_CHUNK
    r = jnp.arange(SCAN_CHUNK)
    ctri = (r[None, :] <= r[:, None]).astype(BF16)

    row_spec = lambda w: pl.BlockSpec((tm, w), lambda i: (i, 0))
    slab = jax.ShapeDtypeStruct((T, rdim), BF16)
    slab32 = jax.ShapeDtypeStruct((T, rdim), F32)
    out_shape = (slab,) * 7 + (jax.ShapeDtypeStruct((T // SCAN_CHUNK, rdim), F32), slab32, slab32)
    out_specs = (row_spec(rdim),) * 7 + (pl.BlockSpec((cpt, rdim), lambda i: (i, 0)), row_spec(rdim), row_spec(rdim))
    consts = (ws, mu, wdu, wau, wgu, dbase, abase, k_k, k_a, r_k, hsum, ctri)
    return pl.pallas_call(
        functools.partial(_rwkv_prep_kernel, tiles_per_seq=seq // tm, rdim=rdim),
        grid=(T // tm,),
        in_specs=[row_spec(D)] + [_const_spec(c.shape) for c in consts],
        out_specs=out_specs,
        out_shape=out_shape,
        scratch_shapes=[pltpu.VMEM((8, ncol), F32)],
        compiler_params=_params("arbitrary"),
        name="rwkv_prep",
    )(x2, *consts)


def _scan_kernel(rt_ref, at_ref, bt_ref, kt_ref, bh_ref, kh_ref, v_ref, gc_ref, y_ref, s_ref):
    C = SCAN_CHUNK
    nb = rt_ref.shape[0]
    npairs = rt_ref.shape[2] // LANES
    chains = [(g, p) for g in range(nb) for p in range(npairs)]

    @pl.when(pl.program_id(1) == 0)
    def _():
        s_ref[...] = jnp.zeros_like(s_ref)

    row = lax.broadcasted_iota(jnp.int32, (2 * C, 1), 0)
    lane = lax.broadcasted_iota(jnp.int32, (1, LANES), 1)
    keep = (row < C) == (lane < HEAD_DIM)
    t_idx = row % C
    s_idx = lane % C
    m_strict = keep & (s_idx < t_idx)
    m_incl = keep & (s_idx <= t_idx)
    eye = jnp.where(keep & (s_idx == t_idx), 1.0, 0.0)
    keep_b = jnp.where(keep, 1.0, 0.0).astype(BF16)

    def stacked(ref, g, p):
        x = ref[g, :, p * LANES:(p + 1) * LANES]
        return jnp.concatenate([x, x], axis=0) * keep_b

    gram = [_dot_nt(jnp.concatenate([stacked(at_ref, g, p), stacked(rt_ref, g, p)], axis=0),
                    jnp.concatenate([stacked(bt_ref, g, p), stacked(kt_ref, g, p)], axis=0)) for g, p in chains]
    a_ab = [jnp.where(m_strict, gm[0:2 * C, 0:LANES], 0.0) for gm in gram]
    a_ak = [jnp.where(m_strict, gm[0:2 * C, LANES:2 * LANES], 0.0).astype(BF16) for gm in gram]
    a_r = [jnp.concatenate([jnp.where(m_incl, gm[2 * C:4 * C, 0:LANES], 0.0),
                            jnp.where(m_incl, gm[2 * C:4 * C, LANES:2 * LANES], 0.0)], axis=1).astype(BF16)
           for gm in gram]

    inv = [eye + a for a in a_ab]
    pw = [a.astype(BF16) for a in a_ab]
    for _ in range(int(math.log2(C)) - 1):
        pw = [_dot(q, q).astype(BF16) for q in pw]
        inv = [x + _dot(x.astype(BF16), q) for x, q in zip(inv, pw)]

    state = [s_ref[i] for i in range(len(chains))]
    state_b = [s.astype(BF16) for s in state]
    rhs = [_dot_nt(stacked(at_ref, g, p), sb) + _dot(ak, stacked(v_ref, g, p))
           for (g, p), sb, ak in zip(chains, state_b, a_ak)]
    u_b = [_dot(x.astype(BF16), r.astype(BF16)).astype(BF16) for x, r in zip(inv, rhs)]
    uv = [jnp.concatenate([u, stacked(v_ref, g, p)], axis=0) for (g, p), u in zip(chains, u_b)]
    y = [_dot_nt(stacked(rt_ref, g, p), sb) + _dot(ar, w) for (g, p), sb, ar, w in zip(chains, state_b, a_r, uv)]
    new_state = [_dot_tn(w, jnp.concatenate([stacked(bh_ref, g, p), stacked(kh_ref, g, p)], axis=0))
                 for (g, p), w in zip(chains, uv)]
    for i, (g, p) in enumerate(chains):
        sl = slice(p * LANES, (p + 1) * LANES)
        y_ref[g, :, sl] = y[i][0:C, :] + y[i][C:2 * C, :]
        s_ref[i] = state[i] * gc_ref[g, 0, :, sl] + new_state[i]


def _rwkv_scan(rt, at, bt, kt, bh, kh, v, gc, batch, seq, nb):
    T, rdim = rt.shape
    C = SCAN_CHUNK
    nchunk = seq // C
    view = lambda t: t.reshape(batch, seq, rdim)
    slab_spec = pl.BlockSpec((nb, C, rdim), lambda b, c: (b, c, 0))
    y = pl.pallas_call(
        _scan_kernel,
        grid=(batch // nb, nchunk),
        in_specs=[slab_spec] * 7 + [pl.BlockSpec((nb, 1, 1, rdim), lambda b, c: (b, c, 0, 0))],
        out_specs=slab_spec,
        out_shape=jax.ShapeDtypeStruct((batch, seq, rdim), F32),
        scratch_shapes=[pltpu.VMEM((nb * rdim // LANES, LANES, LANES), F32)],
        compiler_params=_params("arbitrary", "arbitrary"),
        name="rwkv_scan",
    )(*(view(t) for t in (rt, at, bt, kt, bh, kh, v)), gc.reshape(batch, nchunk, 1, rdim))
    return y.reshape(T, rdim)


def _mla_prep_kernel(x_ref, pos_ref, wm_ref, qg_ref, kvg_ref, wuq_ref, wuk_ref, wuv_ref, freq_ref, sign_ref,
                     q_ref, k_ref, v_ref, *, q_lora, kv_lora, heads):
    z = _dot(x_ref[...].astype(BF16), wm_ref[...])
    c_q = z[:, 0:q_lora]
    c_kv = z[:, q_lora:q_lora + kv_lora]
    kpe = z[:, q_lora + kv_lora:q_lora + kv_lora + LANES]

    def rms(t, gain):
        return t * lax.rsqrt(jnp.mean(t * t, axis=-1, keepdims=True) + RMS_EPS) * gain

    q = _dot(rms(c_q, qg_ref[...]).astype(BF16), wuq_ref[...])
    kvn = rms(c_kv, kvg_ref[...]).astype(BF16)
    kn = _dot(kvn, wuk_ref[...])
    v_ref[0] = _dot_nt(wuv_ref[...], kvn).astype(BF16)

    ang = pos_ref[...].astype(F32) * freq_ref[...]
    cs = jnp.cos(ang)
    sn = jnp.sin(ang) * sign_ref[...]
    lane = lax.broadcasted_iota(jnp.int32, (1, LANES), 1)
    is_x1 = (lane >= HEAD_DIM) & (lane < HEAD_DIM + HALF_ROPE)

    def rope(t):
        partner = jnp.where(is_x1, pltpu.roll(t, LANES - HALF_ROPE, axis=1), pltpu.roll(t, HALF_ROPE, axis=1))
        return t * cs + partner * sn

    kpe_rot = rope(kpe)
    q_scale = (HEAD_DIM + ROPE_DIM) ** -0.5 * math.log2(math.e)
    for h in range(heads):
        sl = slice(h * LANES, (h + 1) * LANES)
        q_ref[:, sl] = (rope(q[:, sl]) * q_scale).astype(BF16)
        k_ref[:, sl] = (kn[:, sl] + kpe_rot).astype(BF16)


def _mla_prep(x2, pos2, wm, qg, kvg, wuq, wuk, wuv, freq, sign, heads, tm):
    T, D = x2.shape
    q_lora, kv_lora = qg.shape[-1], kvg.shape[-1]
    row_spec = lambda w: pl.BlockSpec((tm, w), lambda i: (i, 0))
    consts = (wm, qg, kvg, wuq, wuk, wuv, freq, sign)
    return pl.pallas_call(
        functools.partial(_mla_prep_kernel, q_lora=q_lora, kv_lora=kv_lora, heads=heads),
        grid=(T // tm,),
        in_specs=[row_spec(D), row_spec(1)] + [_const_spec(c.shape) for c in consts],
        out_specs=(row_spec(heads * LANES), row_spec(heads * LANES),
                   pl.BlockSpec((1, heads * HEAD_DIM, tm), lambda i: (i, 0, 0))),
        out_shape=(jax.ShapeDtypeStruct((T, heads * LANES), BF16), jax.ShapeDtypeStruct((T, heads * LANES), BF16),
                   jax.ShapeDtypeStruct((T // tm, heads * HEAD_DIM, tm), BF16)),
        compiler_params=_params("arbitrary"),
        name="mla_prep",
    )(x2, pos2, *consts)


def _attn_kernel(q_ref, k_ref, vt_ref, o_ref, m_ref, l_ref, acc_ref, sa_ref, sb_ref):
    tq = q_ref.shape[0]
    tk = vt_ref.shape[2]
    n = pl.program_id(2)
    m_ref[...] = jnp.full_like(m_ref, NEG_BIG)
    l_ref[...] = jnp.zeros_like(l_ref)
    acc_ref[...] = jnp.zeros_like(acc_ref)
    key_idx = lax.broadcasted_iota(jnp.int32, (tk, 1), 0)
    qry_idx = lax.broadcasted_iota(jnp.int32, (1, tq), 1)

    def scores(t, s_ref):
        kblk = k_ref[pl.ds(pl.multiple_of(t * tk, tk), tk), :]
        for h in range(2):
            sl = slice(h * LANES, (h + 1) * LANES)
            s_ref[h] = _dot_nt(kblk[:, sl], q_ref[:, sl])

    def consume(t, s_ref, visible):
        vt = vt_ref[t]
        for h in range(2):
            s = s_ref[h] if visible is None else jnp.where(visible, s_ref[h], NEG_BIG)
            m_old = m_ref[h]
            m_new = jnp.maximum(m_old, jnp.max(s, axis=0, keepdims=True))
            p = jnp.exp2(s - m_new)
            alpha = jnp.exp2(m_old - m_new)
            l_ref[h] = alpha * l_ref[h] + jnp.sum(p, axis=0, keepdims=True)
            m_ref[h] = m_new
            acc_ref[h] = alpha * acc_ref[h] + _dot(vt[h * HEAD_DIM:(h + 1) * HEAD_DIM, :], p.astype(BF16))

    def trip(i, s_cur, s_nxt):
        @pl.when(i < n)
        def _():
            scores(i + 1, s_nxt)
            consume(i, s_cur, None)

        @pl.when(i == n)
        def _():
            consume(i, s_cur, key_idx <= qry_idx)

    scores(0, sa_ref)

    def body(j, carry):
        i = 2 * j

        @pl.when(i + 1 < n)
        def _():
            scores(i + 1, sb_ref)
            consume(i, sa_ref, None)
            scores(i + 2, sa_ref)
            consume(i + 1, sb_ref, None)

        @pl.when(i + 1 >= n)
        def _():
            trip(i, sa_ref, sb_ref)
            trip(i + 1, sb_ref, sa_ref)

        return carry

    lax.fori_loop(0, (n + 2) // 2, body, 0)
    out_t = jnp.concatenate([acc_ref[0] / l_ref[0], acc_ref[1] / l_ref[1]], axis=0)
    o_ref[...] = out_t.T.astype(o_ref.dtype)


def _mla_attn(q, k, vt, batch, seq, heads, tq):
    T = q.shape[0]
    nq = seq // tq
    tk = vt.shape[2]
    nk = seq // tk
    vt4 = vt.reshape(batch, nk, heads * HEAD_DIM, tk)
    return pl.pallas_call(
        _attn_kernel,
        grid=(batch, heads // 2, nq),
        in_specs=[pl.BlockSpec((tq, 2 * LANES), lambda b, p, i: (b * nq + i, p)),
                  pl.BlockSpec((seq, 2 * LANES), lambda b, p, i: (b, p)),
                  pl.BlockSpec((None, nk, LANES, tk), lambda b, p, i: (b, 0, p, 0))],
        out_specs=pl.BlockSpec((tq, LANES), lambda b, p, i: (b * nq + i, p)),
        out_shape=jax.ShapeDtypeStruct((T, heads * HEAD_DIM), BF16),
        scratch_shapes=[pltpu.VMEM((2, 1, tq), F32), pltpu.VMEM((2, 1, tq), F32),
                        pltpu.VMEM((2, HEAD_DIM, tq), F32),
                        pltpu.VMEM((2, tk, tq), F32), pltpu.VMEM((2, tk, tq), F32)],
        compiler_params=_params("arbitrary", "arbitrary", "arbitrary"),
        name="mla_attn",
    )(q, k, vt4)


def _layer_norm(t, gain, bias):
    mu = jnp.mean(t, axis=-1, keepdims=True)
    c = t - mu
    var = jnp.mean(c * c, axis=-1, keepdims=True)
    return c * lax.rsqrt(var + LN_EPS) * gain + bias


def _mix_kernel(x_ref, y_ref, bonus_ref, g_ref, ym_ref, wg_ref, wpr_ref, wpm_ref, wo_ref, lnxg_ref, lnxb_ref,
                ln1g_ref, ln1b_ref, hsum_ref, h_ref, *, alpha):
    x = x_ref[...]
    d = x.shape[1]
    hsum = hsum_ref[...]
    y = y_ref[...]
    inv_n = 1.0 / HEAD_DIM
    yc = y - _sum_right(y, hsum) * inv_n
    var = _sum_right(yc * yc, hsum) * inv_n
    yn = yc * lax.rsqrt(var + GN_EPS) * lnxg_ref[...] + lnxb_ref[...]
    y_r = ((yn + bonus_ref[...]) * g_ref[...]).astype(BF16)
    gates = _dot(x.astype(BF16), wg_ref[...])
    merged = (jax.nn.sigmoid(gates[:, 0:d]) * _dot(y_r, wpr_ref[...])
              + jax.nn.sigmoid(gates[:, d:2 * d]) * _dot(ym_ref[...], wpm_ref[...]))
    pre = alpha * x + _dot(merged.astype(BF16), wo_ref[...])
    h_ref[...] = _layer_norm(pre, ln1g_ref[...], ln1b_ref[...])


def _mix(x2, y, bonus, g, ym, wg, wpr, wpm, wo, lnxg, lnxb, ln1g, ln1b, hsum, alpha, tm):
    T, D = x2.shape
    rdim = y.shape[1]
    row_spec = lambda w: pl.BlockSpec((tm, w), lambda i: (i, 0))
    consts = (wg, wpr, wpm, wo, lnxg, lnxb, ln1g, ln1b, hsum)
    return pl.pallas_call(
        functools.partial(_mix_kernel, alpha=alpha),
        grid=(T // tm,),
        in_specs=[row_spec(D), row_spec(rdim), row_spec(rdim), row_spec(rdim), row_spec(ym.shape[1])]
        + [_const_spec(c.shape) for c in consts],
        out_specs=row_spec(D),
        out_shape=jax.ShapeDtypeStruct((T, D), F32),
        compiler_params=_params("arbitrary"),
        name="mix",
    )(x2, y, bonus, g, ym, *consts)


def _ffn_kernel(h_ref, wg_ref, wu_ref, wd_ref, ln2g_ref, ln2b_ref, o_ref, *, alpha, ff_bounds):
    h = h_ref[...]
    hb = h.astype(BF16)
    acc = alpha * h
    for lo, hi in zip(ff_bounds[:-1], ff_bounds[1:]):
        sl = slice(lo, hi)
        gate = _dot(hb, wg_ref[:, sl])
        act = (gate * jax.nn.sigmoid(gate) * _dot(hb, wu_ref[:, sl])).astype(BF16)
        acc = acc + _dot(act, wd_ref[sl, :])
    o_ref[...] = _layer_norm(acc, ln2g_ref[...], ln2b_ref[...])


def _ffn(h, wg, wu, wd, ln2g, ln2b, alpha, tm):
    T, D = h.shape
    dff = wg.shape[1]
    split = min(dff, -(-(dff // 2) // MXU_TILE) * MXU_TILE)
    ff_bounds = (0, split, dff) if split < dff else (0, dff)
    row_spec = pl.BlockSpec((tm, D), lambda i: (i, 0))
    consts = (wg, wu, wd, ln2g, ln2b)
    return pl.pallas_call(
        functools.partial(_ffn_kernel, alpha=alpha, ff_bounds=ff_bounds),
        grid=(T // tm,),
        in_specs=[row_spec] + [_const_spec(c.shape) for c in consts],
        out_specs=row_spec,
        out_shape=jax.ShapeDtypeStruct((T, D), F32),
        compiler_params=_params("arbitrary"),
        name="ffn",
    )(h, *consts)


def _layer(x2, pos2, batch, seq, alpha, w_in, mu_shift, w_decay_up, w_decay_base, w_aaa_up, w_aaa_base, w_gate_up,
           k_k, k_a, r_k, lnx_g, lnx_b, q_norm_g, w_uq, kv_norm_g, w_ukv, w_proj_rwkv, w_proj_mla, w_out,
           ln1_g, ln1_b, w_ffn_gate, w_ffn_up, w_ffn_down, ln2_g, ln2_b):
    D = x2.shape[1]
    rdim = k_k.shape[0]
    dl, al, gl = w_decay_up.shape[0], w_aaa_up.shape[0], w_gate_up.shape[0]
    q_lora, kv_lora = q_norm_g.shape[0], kv_norm_g.shape[0]
    heads = w_uq.shape[1] // (HEAD_DIM + ROPE_DIM)
    assert rdim % LANES == 0 and dl + al == LANES and gl == LANES and heads % 2 == 0
    assert w_ukv.shape[1] == heads * 2 * HEAD_DIM and w_proj_mla.shape[0] == heads * HEAD_DIM
    shift_cols = 3 * rdim + dl + al + gl
    tm = min(512, seq)
    assert seq % tm == 0 and tm % SCAN_CHUNK == 0
    row = lambda t: t.reshape(1, -1).astype(F32)

    ws = w_in[:, :shift_cols].astype(BF16)
    wdu = jnp.concatenate([w_decay_up, jnp.zeros((al, rdim), F32)], axis=0).astype(BF16)
    wau = jnp.concatenate([jnp.zeros((dl, rdim), F32), w_aaa_up], axis=0).astype(BF16)
    hid = jnp.arange(rdim) // HEAD_DIM
    hsum = (hid[:, None] == hid[None, :]).astype(BF16)
    rt, at, bt, kt, bh, kh, v, gc, bonus, g = _rwkv_prep(
        x2, ws, row(mu_shift), wdu, wau, w_gate_up.astype(BF16), row(w_decay_base), row(w_aaa_base), row(k_k),
        row(k_a), row(r_k), hsum, seq, tm)
    y = _rwkv_scan(rt, at, bt, kt, bh, kh, v, gc, batch, seq, 4 if batch % 4 == 0 else 1)

    o = shift_cols
    w_cq, w_ckv = w_in[:, o:o + q_lora], w_in[:, o + q_lora:o + q_lora + kv_lora]
    w_kpe = w_in[:, o + q_lora + kv_lora:o + q_lora + kv_lora + ROPE_DIM]
    pad_r = LANES - HEAD_DIM - ROPE_DIM
    wm = jnp.concatenate([w_cq, w_ckv, jnp.zeros((D, HEAD_DIM), F32), w_kpe, jnp.zeros((D, pad_r), F32)],
                         axis=1).astype(BF16)
    wuq = jnp.pad(w_uq.reshape(q_lora, heads, HEAD_DIM + ROPE_DIM), ((0, 0), (0, 0), (0, pad_r)))
    wuq = wuq.reshape(q_lora, heads * LANES).astype(BF16)
    ukv = w_ukv.reshape(kv_lora, heads, 2 * HEAD_DIM)
    wuk = jnp.pad(ukv[:, :, :HEAD_DIM], ((0, 0), (0, 0), (0, LANES - HEAD_DIM))).reshape(kv_lora, heads * LANES)
    wuv = ukv[:, :, HEAD_DIM:].reshape(kv_lora, heads * HEAD_DIM).T
    inv_freq = ROPE_THETA ** (-jnp.arange(0, ROPE_DIM, 2, dtype=F32) / ROPE_DIM)
    zeros = lambda n: jnp.zeros((n,), F32)
    freq = jnp.concatenate([zeros(HEAD_DIM), inv_freq, inv_freq, zeros(pad_r)]).reshape(1, LANES)
    sign = jnp.concatenate([zeros(HEAD_DIM), -jnp.ones((HALF_ROPE,), F32), jnp.ones((HALF_ROPE,), F32),
                            zeros(pad_r)]).reshape(1, LANES)
    q, k, vv = _mla_prep(x2, pos2, wm, row(q_norm_g), row(kv_norm_g), wuq, wuk.astype(BF16), wuv.astype(BF16),
                         freq, sign, heads, tm)
    ym = _mla_attn(q, k, vv, batch, seq, heads, tm)

    o2 = o + q_lora + kv_lora + ROPE_DIM
    wg = w_in[:, o2:o2 + 2 * D].astype(BF16)
    h = _mix(x2, y, bonus, g, ym, wg, w_proj_rwkv.astype(BF16), w_proj_mla.astype(BF16), w_out.astype(BF16),
             row(lnx_g), row(lnx_b), row(ln1_g), row(ln1_b), hsum, alpha, tm)
    return _ffn(h, w_ffn_gate.astype(BF16), w_ffn_up.astype(BF16), w_ffn_down.astype(BF16), row(ln2_g), row(ln2_b),
                alpha, tm)


def kernel(x, positions, w_in, mu_shift, w_decay_up, w_decay_base, w_aaa_up, w_aaa_base, w_gate_up, k_k, k_a, r_k,
           lnx_g, lnx_b, q_norm_g, w_uq, kv_norm_g, w_ukv, w_proj_rwkv, w_proj_mla, w_out, ln1_g, ln1_b,
           w_ffn_gate, w_ffn_up, w_ffn_down, ln2_g, ln2_b):
    batch, seq, D = x.shape
    depth = w_in.shape[0]
    alpha = (2.0 * depth) ** 0.25
    h = x.reshape(batch * seq, D)
    pos2 = positions.reshape(batch * seq, 1)
    per_layer = (w_in, mu_shift, w_decay_up, w_decay_base, w_aaa_up, w_aaa_base, w_gate_up, k_k, k_a, r_k, lnx_g,
                 lnx_b, q_norm_g, w_uq, kv_norm_g, w_ukv, w_proj_rwkv, w_proj_mla, w_out, ln1_g, ln1_b, w_ffn_gate,
                 w_ffn_up, w_ffn_down, ln2_g, ln2_b)
    for l in range(depth):
        h = _layer(h, pos2, batch, seq, alpha, *(p[l] for p in per_layer))
    return h.reshape(batch, seq, D)
```

```python
import functools
import math

import jax
import jax.numpy as jnp
from jax import lax
from jax.experimental import pallas as pl
from jax.experimental.pallas import tpu as pltpu

F32 = jnp.float32
BF16 = jnp.bfloat16

HEAD_DIM = 64
ROPE_DIM = 32
HALF_ROPE = ROPE_DIM // 2
ROPE_THETA = 10000.0
GN_EPS = 64e-5
LN_EPS = 1e-5
RMS_EPS = 1e-6
LANES = 128
MXU_TILE = 256
SCAN_CHUNK = 64
VMEM_LIMIT = 56 * 1024 * 1024
NEG_BIG = -1e30


def _dot(a, b):
    return jnp.dot(a, b, preferred_element_type=F32)


def _dot_nt(a, b):
    return lax.dot_general(a, b, (((1,), (1,)), ((), ())), preferred_element_type=F32)


def _dot_tn(a, b):
    return lax.dot_general(a, b, (((0,), (0,)), ((), ())), preferred_element_type=F32)


def _split3(x):
    hi = x.astype(BF16)
    r1 = x - hi.astype(F32)
    mid = r1.astype(BF16)
    lo = (r1 - mid.astype(F32)).astype(BF16)
    return hi, mid, lo


def _head_sums(x, ones_blk):
    hi = x.astype(BF16)
    lo = (x - hi.astype(F32)).astype(BF16)
    w = ones_blk.shape[0]
    cols = [slice(c, c + w) for c in range(0, x.shape[1], w)]
    return jnp.concatenate([_dot(hi[:, c], ones_blk) + _dot(lo[:, c], ones_blk) for c in cols], axis=1)


def _sum_left(m, x):
    hi, mid, lo = _split3(x)
    return _dot(m, hi) + _dot(m, mid) + _dot(m, lo)


def _const_spec(shape):
    nd = len(shape)
    return pl.BlockSpec(shape, lambda *_: (0,) * nd, pipeline_mode=pl.Buffered(1))


def _params(*sem):
    return pltpu.CompilerParams(dimension_semantics=sem, vmem_limit_bytes=VMEM_LIMIT)


def _rwkv_prep_kernel(x_ref, ws_ref, mu_ref, wdu_ref, wau_ref, wgu_ref, dbase_ref, abase_ref, kk_ref, ka_ref,
                      rk_ref, hsum_ref, ctri_ref,
                      rt_ref, at_ref, bt_ref, kt_ref, bh_ref, kh_ref, v_ref, gc_ref, bonus_ref, g_ref,
                      carry_ref, *, tiles_per_seq, rdim):
    tm = x_ref.shape[0]
    first = (pl.program_id(0) % tiles_per_seq) == 0

    @pl.when(first)
    def _():
        carry_ref[...] = jnp.zeros_like(carry_ref)

    th = tm // 2
    parts = [slice(0, th), slice(th, tm)]
    z = [_dot(x_ref[p, :].astype(BF16), ws_ref[...]) for p in parts]
    prev_last = [carry_ref[0:1, :], z[0][th - 1:th, :]]
    carry_ref[0:1, :] = z[1][th - 1:th, :]
    row = lax.broadcasted_iota(jnp.int32, (th, 1), 0)
    zs = []
    for zi, prev in zip(z, prev_last):
        z_prev = jnp.where(row == 0, prev, pltpu.roll(zi, 1, axis=0))
        zs.append(zi + mu_ref[...] * (z_prev - zi))
    z_r = [t[:, 0:rdim] for t in zs]
    z_k = [t[:, rdim:2 * rdim] for t in zs]
    z_v = [t[:, 2 * rdim:3 * rdim] for t in zs]
    z_lora = [t[:, 3 * rdim:3 * rdim + LANES] for t in zs]
    z_gd = [t[:, 3 * rdim + LANES:3 * rdim + 2 * LANES] for t in zs]

    d_up = [_dot(jnp.tanh(t).astype(BF16), wdu_ref[...]) for t in z_lora]
    a_up = [_dot(t.astype(BF16), wau_ref[...]) for t in z_lora]
    g_up = [_dot(jax.nn.sigmoid(t).astype(BF16), wgu_ref[...]) for t in z_gd]
    hsum = hsum_ref[...]
    kk_raw = [t * kk_ref[...] for t in z_k]
    kk_ss = [_head_sums(t * t, hsum) for t in kk_raw]
    a = [jax.nn.sigmoid(abase_ref[...] + t) for t in a_up]
    kmod = [zk * (1.0 + (ai - 1.0) * ka_ref[...]) for zk, ai in zip(z_k, a)]
    bonus = [_head_sums(zr * km * rk_ref[...], hsum) for zr, km in zip(z_r, kmod)]

    ctri = ctri_ref[...]
    C = ctri.shape[0]
    for i, p in enumerate(parts):
        t = -(dbase_ref[...] + d_up[i])
        softplus = jnp.maximum(t, 0.0) + jnp.log1p(jnp.exp(-jnp.abs(t)))
        logw = -jnp.exp(-softplus - 0.5)
        kk = kk_raw[i] / jnp.maximum(jnp.sqrt(kk_ss[i]), 1e-12)
        b = kk * a[i]
        g_ref[p, :] = g_up[i]
        bonus_ref[p, :] = bonus[i] * z_v[i]
        v_ref[p, :] = z_v[i].astype(BF16)
        for c in range(th // C):
            sl = slice(c * C, (c + 1) * C)
            out = slice(p.start + c * C, p.start + (c + 1) * C)
            lw = logw[sl, :]
            cum = _sum_left(ctri, lw)
            e_neg = jnp.exp(-cum)
            chunk_decay = jnp.exp(cum[C - 1:C, :])
            gc_ref[out.start // C:out.start // C + 1, :] = chunk_decay
            b_neg = b[sl, :] * e_neg
            k_neg = kmod[i][sl, :] * e_neg
            rt_ref[out, :] = (z_r[i][sl, :] * jnp.exp(cum)).astype(BF16)
            at_ref[out, :] = (-kk[sl, :] * jnp.exp(cum - lw)).astype(BF16)
            bt_ref[out, :] = b_neg.astype(BF16)
            kt_ref[out, :] = k_neg.astype(BF16)
            bh_ref[out, :] = (b_neg * chunk_decay).astype(BF16)
            kh_ref[out, :] = (k_neg * chunk_decay).astype(BF16)


def _rwkv_prep(x2, ws, mu, wdu, wau, wgu, dbase, abase, k_k, k_a, r_k, hsum, seq, tm):
    T, D = x2.shape
    rdim = k_k.shape[-1]
    ncol = ws.shape[1]
    cpt = tm // SCAN_CHUNK
    r = jnp.arange(SCAN_CHUNK)
    ctri = (r[None, :] <= r[:, None]).astype(BF16)

    row_spec = lambda w: pl.BlockSpec((tm, w), lambda i: (i, 0))
    slab = jax.ShapeDtypeStruct((T, rdim), BF16)
    slab32 = jax.ShapeDtypeStruct((T, rdim), F32)
    out_shape = (slab,) * 7 + (jax.ShapeDtypeStruct((T // SCAN_CHUNK, rdim), F32), slab32, slab32)
    out_specs = (row_spec(rdim),) * 7 + (pl.BlockSpec((cpt, rdim), lambda i: (i, 0)), row_spec(rdim), row_spec(rdim))
    consts = (ws, mu, wdu, wau, wgu, dbase, abase, k_k, k_a, r_k, hsum, ctri)
    return pl.pallas_call(
        functools.partial(_rwkv_prep_kernel, tiles_per_seq=seq // tm, rdim=rdim),
        grid=(T // tm,),
        in_specs=[row_spec(D)] + [_const_spec(c.shape) for c in consts],
        out_specs=out_specs,
        out_shape=out_shape,
        scratch_shapes=[pltpu.VMEM((8, ncol), F32)],
        compiler_params=_params("arbitrary"),
        name="rwkv_prep",
    )(x2, *consts)


def _scan_kernel(rt_ref, at_ref, bt_ref, kt_ref, bh_ref, kh_ref, v_ref, gc_ref, y_ref, s_ref):
    C = SCAN_CHUNK
    nb = rt_ref.shape[0]
    npairs = rt_ref.shape[2] // LANES
    chains = [(g, p) for g in range(nb) for p in range(npairs)]

    @pl.when(pl.program_id(1) == 0)
    def _():
        s_ref[...] = jnp.zeros_like(s_ref)

    row = lax.broadcasted_iota(jnp.int32, (2 * C, 1), 0)
    lane = lax.broadcasted_iota(jnp.int32, (1, LANES), 1)
    keep = (row < C) == (lane < HEAD_DIM)
    t_idx = row % C
    s_idx = lane % C
    m_strict = keep & (s_idx < t_idx)
    m_incl = keep & (s_idx <= t_idx)
    eye = jnp.where(keep & (s_idx == t_idx), 1.0, 0.0)
    keep_b = jnp.where(keep, 1.0, 0.0).astype(BF16)

    def stacked(ref, g, p):
        x = ref[g, :, p * LANES:(p + 1) * LANES]
        return jnp.concatenate([x, x], axis=0) * keep_b

    gram = [_dot_nt(jnp.concatenate([stacked(at_ref, g, p), stacked(rt_ref, g, p)], axis=0),
                    jnp.concatenate([stacked(bt_ref, g, p), stacked(kt_ref, g, p)], axis=0)) for g, p in chains]
    a_ab = [jnp.where(m_strict, gm[0:2 * C, 0:LANES], 0.0) for gm in gram]
    a_ak = [jnp.where(m_strict, gm[0:2 * C, LANES:2 * LANES], 0.0).astype(BF16) for gm in gram]
    a_r = [jnp.concatenate([jnp.where(m_incl, gm[2 * C:4 * C, 0:LANES], 0.0),
                            jnp.where(m_incl, gm[2 * C:4 * C, LANES:2 * LANES], 0.0)], axis=1).astype(BF16)
           for gm in gram]

    inv = [eye + a for a in a_ab]
    pw = [a.astype(BF16) for a in a_ab]
    for _ in range(int(math.log2(C)) - 1):
        pw = [_dot(q, q).astype(BF16) for q in pw]
        inv = [x + _dot(x.astype(BF16), q) for x, q in zip(inv, pw)]

    state = [s_ref[i] for i in range(len(chains))]
    state_b = [s.astype(BF16) for s in state]
    rhs = [_dot_nt(stacked(at_ref, g, p), sb) + _dot(ak, stacked(v_ref, g, p))
           for (g, p), sb, ak in zip(chains, state_b, a_ak)]
    u_b = [_dot(x.astype(BF16), r.astype(BF16)).astype(BF16) for x, r in zip(inv, rhs)]
    uv = [jnp.concatenate([u, stacked(v_ref, g, p)], axis=0) for (g, p), u in zip(chains, u_b)]
    y = [_dot_nt(stacked(rt_ref, g, p), sb) + _dot(ar, w) for (g, p), sb, ar, w in zip(chains, state_b, a_r, uv)]
    new_state = [_dot_tn(w, jnp.concatenate([stacked(bh_ref, g, p), stacked(kh_ref, g, p)], axis=0))
                 for (g, p), w in zip(chains, uv)]
    for i, (g, p) in enumerate(chains):
        sl = slice(p * LANES, (p + 1) * LANES)
        y_ref[g, :, sl] = y[i][0:C, :] + y[i][C:2 * C, :]
        s_ref[i] = state[i] * gc_ref[g, 0, :, sl] + new_state[i]


def _rwkv_scan(rt, at, bt, kt, bh, kh, v, gc, batch, seq, nb):
    T, rdim = rt.shape
    C = SCAN_CHUNK
    nchunk = seq // C
    view = lambda t: t.reshape(batch, seq, rdim)
    slab_spec = pl.BlockSpec((nb, C, rdim), lambda b, c: (b, c, 0))
    y = pl.pallas_call(
        _scan_kernel,
        grid=(batch // nb, nchunk),
        in_specs=[slab_spec] * 7 + [pl.BlockSpec((nb, 1, 1, rdim), lambda b, c: (b, c, 0, 0))],
        out_specs=slab_spec,
        out_shape=jax.ShapeDtypeStruct((batch, seq, rdim), F32),
        scratch_shapes=[pltpu.VMEM((nb * rdim // LANES, LANES, LANES), F32)],
        compiler_params=_params("arbitrary", "arbitrary"),
        name="rwkv_scan",
    )(*(view(t) for t in (rt, at, bt, kt, bh, kh, v)), gc.reshape(batch, nchunk, 1, rdim))
    return y.reshape(T, rdim)


def _mla_prep_kernel(x_ref, pos_ref, wm_ref, qg_ref, kvg_ref, wuq_ref, wuk_ref, wuv_ref, freq_ref, sign_ref,
                     q_ref, k_ref, v_ref, *, q_lora, kv_lora, heads):
    tm = x_ref.shape[0]
    parts = [slice(i * (tm // 2), (i + 1) * (tm // 2)) for i in range(2)]

    def rms(t, gain):
        return t * lax.rsqrt(jnp.mean(t * t, axis=-1, keepdims=True) + RMS_EPS) * gain

    z = [_dot(x_ref[p, :].astype(BF16), wm_ref[...]) for p in parts]
    qn = [rms(t[:, 0:q_lora], qg_ref[...]).astype(BF16) for t in z]
    kvn = [rms(t[:, q_lora:q_lora + kv_lora], kvg_ref[...]).astype(BF16) for t in z]
    q = [_dot(t, wuq_ref[...]) for t in qn]
    kn = [_dot(t, wuk_ref[...]) for t in kvn]
    vt = [_dot_nt(wuv_ref[...], t) for t in kvn]

    lane = lax.broadcasted_iota(jnp.int32, (1, LANES), 1)
    is_x1 = (lane >= HEAD_DIM) & (lane < HEAD_DIM + HALF_ROPE)
    q_scale = (HEAD_DIM + ROPE_DIM) ** -0.5 * math.log2(math.e)
    for i, p in enumerate(parts):
        v_ref[0, :, p] = vt[i].astype(BF16)
        ang = pos_ref[p, :].astype(F32) * freq_ref[...]
        cs = jnp.cos(ang)
        sn = jnp.sin(ang) * sign_ref[...]

        def rope(t):
            partner = jnp.where(is_x1, pltpu.roll(t, LANES - HALF_ROPE, axis=1), pltpu.roll(t, HALF_ROPE, axis=1))
            return t * cs + partner * sn

        kpe_rot = rope(z[i][:, q_lora + kv_lora:q_lora + kv_lora + LANES])
        for h in range(heads):
            sl = slice(h * LANES, (h + 1) * LANES)
            q_ref[p, sl] = (rope(q[i][:, sl]) * q_scale).astype(BF16)
            k_ref[p, sl] = (kn[i][:, sl] + kpe_rot).astype(BF16)


def _mla_prep(x2, pos2, wm, qg, kvg, wuq, wuk, wuv, freq, sign, heads, tm):
    T, D = x2.shape
    q_lora, kv_lora = qg.shape[-1], kvg.shape[-1]
    row_spec = lambda w: pl.BlockSpec((tm, w), lambda i: (i, 0))
    consts = (wm, qg, kvg, wuq, wuk, wuv, freq, sign)
    return pl.pallas_call(
        functools.partial(_mla_prep_kernel, q_lora=q_lora, kv_lora=kv_lora, heads=heads),
        grid=(T // tm,),
        in_specs=[row_spec(D), row_spec(1)] + [_const_spec(c.shape) for c in consts],
        out_specs=(row_spec(heads * LANES), row_spec(heads * LANES),
                   pl.BlockSpec((1, heads * HEAD_DIM, tm), lambda i: (i, 0, 0))),
        out_shape=(jax.ShapeDtypeStruct((T, heads * LANES), BF16), jax.ShapeDtypeStruct((T, heads * LANES), BF16),
                   jax.ShapeDtypeStruct((T // tm, heads * HEAD_DIM, tm), BF16)),
        compiler_params=_params("arbitrary"),
        name="mla_prep",
    )(x2, pos2, *consts)


def _attn_kernel(q_ref, k_ref, vt_ref, o_ref, m_ref, l_ref, acc_ref, sa_ref, sb_ref):
    tk = vt_ref.shape[2]
    tq = tk
    nq = q_ref.shape[0] // tq
    key_idx = lax.broadcasted_iota(jnp.int32, (tk, 1), 0)
    qry_idx = lax.broadcasted_iota(jnp.int32, (1, tq), 1)

    def reset():
        m_ref[...] = jnp.full_like(m_ref, NEG_BIG)
        l_ref[...] = jnp.zeros_like(l_ref)
        acc_ref[...] = jnp.zeros_like(acc_ref)

    def scores(qi, t, s_ref):
        qi = jnp.minimum(qi, nq - 1)
        kblk = k_ref[pl.ds(pl.multiple_of(t * tk, tk), tk), :]
        qblk = q_ref[pl.ds(pl.multiple_of(qi * tq, tq), tq), :]
        for h in range(2):
            sl = slice(h * LANES, (h + 1) * LANES)
            s_ref[h] = _dot_nt(kblk[:, sl], qblk[:, sl])

    def consume(t, s_ref, visible):
        vt = vt_ref[t]
        for h in range(2):
            s = s_ref[h] if visible is None else jnp.where(visible, s_ref[h], NEG_BIG)
            m_old = m_ref[h]
            m_new = jnp.maximum(m_old, jnp.max(s, axis=0, keepdims=True))
            p = jnp.exp2(s - m_new)
            alpha = jnp.exp2(m_old - m_new)
            l_ref[h] = alpha * l_ref[h] + jnp.sum(p, axis=0, keepdims=True)
            m_ref[h] = m_new
            acc_ref[h] = alpha * acc_ref[h] + _dot(vt[h * HEAD_DIM:(h + 1) * HEAD_DIM, :], p.astype(BF16))

    def finish(qi):
        out_t = jnp.concatenate([acc_ref[0] / l_ref[0], acc_ref[1] / l_ref[1]], axis=0)
        o_ref[pl.ds(pl.multiple_of(qi * tq, tq), tq), :] = out_t.T.astype(o_ref.dtype)
        reset()

    def succ(qi, t):
        last = t == qi
        return jnp.where(last, qi + 1, qi), jnp.where(last, 0, t + 1)

    def close(qi, t, s_ref, diagonal):
        consume(t, s_ref, key_idx <= qry_idx if diagonal else None)
        if diagonal:
            finish(qi)

    reset()
    scores(0, 0, sa_ref)

    def body(j, carry):
        qi, t = carry
        q1, t1 = succ(qi, t)
        q2, t2 = succ(q1, t1)
        for d0 in (False, True):
            for d1 in (False, True):
                @pl.when(((t == qi) == d0) & ((t1 == q1) == d1))
                def _(d0=d0, d1=d1):
                    scores(q1, t1, sb_ref)
                    close(qi, t, sa_ref, d0)
                    scores(q2, t2, sa_ref)
                    close(q1, t1, sb_ref, d1)
        return q2, t2

    total = nq * (nq + 1) // 2
    lax.fori_loop(0, total // 2, body, (jnp.int32(0), jnp.int32(0)))
    if total % 2:
        close(jnp.int32(nq - 1), jnp.int32(nq - 1), sa_ref, True)


def _mla_attn(q, k, vt, batch, seq, heads):
    T = q.shape[0]
    tk = vt.shape[2]
    nk = seq // tk
    vt4 = vt.reshape(batch, nk, heads * HEAD_DIM, tk)
    return pl.pallas_call(
        _attn_kernel,
        grid=(batch, heads // 2),
        in_specs=[pl.BlockSpec((seq, 2 * LANES), lambda b, p: (b, p)),
                  pl.BlockSpec((seq, 2 * LANES), lambda b, p: (b, p)),
                  pl.BlockSpec((None, nk, LANES, tk), lambda b, p: (b, 0, p, 0))],
        out_specs=pl.BlockSpec((seq, LANES), lambda b, p: (b, p)),
        out_shape=jax.ShapeDtypeStruct((T, heads * HEAD_DIM), BF16),
        scratch_shapes=[pltpu.VMEM((2, 1, tk), F32), pltpu.VMEM((2, 1, tk), F32),
                        pltpu.VMEM((2, HEAD_DIM, tk), F32),
                        pltpu.VMEM((2, tk, tk), F32), pltpu.VMEM((2, tk, tk), F32)],
        compiler_params=_params("arbitrary", "arbitrary"),
        name="mla_attn",
    )(q, k, vt4)


def _layer_norm(t, gain, bias):
    mu = jnp.mean(t, axis=-1, keepdims=True)
    c = t - mu
    var = jnp.mean(c * c, axis=-1, keepdims=True)
    return c * lax.rsqrt(var + LN_EPS) * gain + bias


def _mix_kernel(x_ref, y_ref, bonus_ref, g_ref, ym_ref, wg_ref, wpr_ref, wpm_ref, wo_ref, lnxg_ref, lnxb_ref,
                ln1g_ref, ln1b_ref, hsum_ref, h_ref, *, alpha):
    tm, d = x_ref.shape
    parts = [slice(0, tm // 2), slice(tm // 2, tm)]
    hsum = hsum_ref[...]
    inv_n = 1.0 / HEAD_DIM
    gates = [_dot(x_ref[p, :].astype(BF16), wg_ref[...]) for p in parts]
    pm = [_dot(ym_ref[p, :], wpm_ref[...]) for p in parts]
    mean = [_head_sums(y_ref[p, :], hsum) * inv_n for p in parts]
    yc = [y_ref[p, :] - mu for p, mu in zip(parts, mean)]
    var = [_head_sums(c * c, hsum) * inv_n for c in yc]
    y_r = [(((c * lax.rsqrt(v + GN_EPS) * lnxg_ref[...] + lnxb_ref[...]) + bonus_ref[p, :]) * g_ref[p, :]).astype(BF16)
           for p, c, v in zip(parts, yc, var)]
    pr = [_dot(t, wpr_ref[...]) for t in y_r]
    merged = [(jax.nn.sigmoid(gt[:, 0:d]) * a + jax.nn.sigmoid(gt[:, d:2 * d]) * b).astype(BF16)
              for gt, a, b in zip(gates, pr, pm)]
    out = [_dot(t, wo_ref[...]) for t in merged]
    for p, o in zip(parts, out):
        h_ref[p, :] = _layer_norm(alpha * x_ref[p, :] + o, ln1g_ref[...], ln1b_ref[...])


def _mix(x2, y, bonus, g, ym, wg, wpr, wpm, wo, lnxg, lnxb, ln1g, ln1b, hsum, alpha, tm):
    T, D = x2.shape
    rdim = y.shape[1]
    row_spec = lambda w: pl.BlockSpec((tm, w), lambda i: (i, 0))
    consts = (wg, wpr, wpm, wo, lnxg, lnxb, ln1g, ln1b, hsum)
    return pl.pallas_call(
        functools.partial(_mix_kernel, alpha=alpha),
        grid=(T // tm,),
        in_specs=[row_spec(D), row_spec(rdim), row_spec(rdim), row_spec(rdim), row_spec(ym.shape[1])]
        + [_const_spec(c.shape) for c in consts],
        out_specs=row_spec(D),
        out_shape=jax.ShapeDtypeStruct((T, D), F32),
        compiler_params=_params("arbitrary"),
        name="mix",
    )(x2, y, bonus, g, ym, *consts)


def _ffn_kernel(h_ref, wg_ref, wu_ref, wd_ref, ln2g_ref, ln2b_ref, o_ref, *, alpha, ff_bounds):
    h = h_ref[...]
    hb = h.astype(BF16)
    acc = alpha * h
    for lo, hi in zip(ff_bounds[:-1], ff_bounds[1:]):
        sl = slice(lo, hi)
        gate = _dot(hb, wg_ref[:, sl])
        act = (gate * jax.nn.sigmoid(gate) * _dot(hb, wu_ref[:, sl])).astype(BF16)
        acc = acc + _dot(act, wd_ref[sl, :])
    o_ref[...] = _layer_norm(acc, ln2g_ref[...], ln2b_ref[...])


def _ffn(h, wg, wu, wd, ln2g, ln2b, alpha, tm):
    T, D = h.shape
    dff = wg.shape[1]
    split = min(dff, -(-(dff // 2) // MXU_TILE) * MXU_TILE)
    ff_bounds = (0, split, dff) if split < dff else (0, dff)
    row_spec = pl.BlockSpec((tm, D), lambda i: (i, 0))
    consts = (wg, wu, wd, ln2g, ln2b)
    return pl.pallas_call(
        functools.partial(_ffn_kernel, alpha=alpha, ff_bounds=ff_bounds),
        grid=(T // tm,),
        in_specs=[row_spec] + [_const_spec(c.shape) for c in consts],
        out_specs=row_spec,
        out_shape=jax.ShapeDtypeStruct((T, D), F32),
        compiler_params=_params("arbitrary"),
        name="ffn",
    )(h, *consts)


def _layer(x2, pos2, batch, seq, alpha, w_in, mu_shift, w_decay_up, w_decay_base, w_aaa_up, w_aaa_base, w_gate_up,
           k_k, k_a, r_k, lnx_g, lnx_b, q_norm_g, w_uq, kv_norm_g, w_ukv, w_proj_rwkv, w_proj_mla, w_out,
           ln1_g, ln1_b, w_ffn_gate, w_ffn_up, w_ffn_down, ln2_g, ln2_b):
    D = x2.shape[1]
    rdim = k_k.shape[0]
    dl, al, gl = w_decay_up.shape[0], w_aaa_up.shape[0], w_gate_up.shape[0]
    q_lora, kv_lora = q_norm_g.shape[0], kv_norm_g.shape[0]
    heads = w_uq.shape[1] // (HEAD_DIM + ROPE_DIM)
    assert rdim % LANES == 0 and dl + al == LANES and gl == LANES and heads % 2 == 0
    assert w_ukv.shape[1] == heads * 2 * HEAD_DIM and w_proj_mla.shape[0] == heads * HEAD_DIM
    shift_cols = 3 * rdim + dl + al + gl
    tm = min(512, seq)
    assert seq % tm == 0 and tm % SCAN_CHUNK == 0
    row = lambda t: t.reshape(1, -1).astype(F32)

    ws = w_in[:, :shift_cols].astype(BF16)
    wdu = jnp.concatenate([w_decay_up, jnp.zeros((al, rdim), F32)], axis=0).astype(BF16)
    wau = jnp.concatenate([jnp.zeros((dl, rdim), F32), w_aaa_up], axis=0).astype(BF16)
    assert rdim % MXU_TILE == 0
    hid = jnp.arange(MXU_TILE) // HEAD_DIM
    hsum = (hid[:, None] == hid[None, :]).astype(BF16)
    rt, at, bt, kt, bh, kh, v, gc, bonus, g = _rwkv_prep(
        x2, ws, row(mu_shift), wdu, wau, w_gate_up.astype(BF16), row(w_decay_base), row(w_aaa_base), row(k_k),
        row(k_a), row(r_k), hsum, seq, tm)
    y = _rwkv_scan(rt, at, bt, kt, bh, kh, v, gc, batch, seq, 4 if batch % 4 == 0 else 1)

    o = shift_cols
    w_cq, w_ckv = w_in[:, o:o + q_lora], w_in[:, o + q_lora:o + q_lora + kv_lora]
    w_kpe = w_in[:, o + q_lora + kv_lora:o + q_lora + kv_lora + ROPE_DIM]
    pad_r = LANES - HEAD_DIM - ROPE_DIM
    wm = jnp.concatenate([w_cq, w_ckv, jnp.zeros((D, HEAD_DIM), F32), w_kpe, jnp.zeros((D, pad_r), F32)],
                         axis=1).astype(BF16)
    wuq = jnp.pad(w_uq.reshape(q_lora, heads, HEAD_DIM + ROPE_DIM), ((0, 0), (0, 0), (0, pad_r)))
    wuq = wuq.reshape(q_lora, heads * LANES).astype(BF16)
    ukv = w_ukv.reshape(kv_lora, heads, 2 * HEAD_DIM)
    wuk = jnp.pad(ukv[:, :, :HEAD_DIM], ((0, 0), (0, 0), (0, LANES - HEAD_DIM))).reshape(kv_lora, heads * LANES)
    wuv = ukv[:, :, HEAD_DIM:].reshape(kv_lora, heads * HEAD_DIM).T
    inv_freq = ROPE_THETA ** (-jnp.arange(0, ROPE_DIM, 2, dtype=F32) / ROPE_DIM)
    zeros = lambda n: jnp.zeros((n,), F32)
    freq = jnp.concatenate([zeros(HEAD_DIM), inv_freq, inv_freq, zeros(pad_r)]).reshape(1, LANES)
    sign = jnp.concatenate([zeros(HEAD_DIM), -jnp.ones((HALF_ROPE,), F32), jnp.ones((HALF_ROPE,), F32),
                            zeros(pad_r)]).reshape(1, LANES)
    q, k, vv = _mla_prep(x2, pos2, wm, row(q_norm_g), row(kv_norm_g), wuq, wuk.astype(BF16), wuv.astype(BF16),
                         freq, sign, heads, tm)
    ym = _mla_attn(q, k, vv, batch, seq, heads)

    o2 = o + q_lora + kv_lora + ROPE_DIM
    wg = w_in[:, o2:o2 + 2 * D].astype(BF16)
    h = _mix(x2, y, bonus, g, ym, wg, w_proj_rwkv.astype(BF16), w_proj_mla.astype(BF16), w_out.astype(BF16),
             row(lnx_g), row(lnx_b), row(ln1_g), row(ln1_b), hsum, alpha, tm)
    return _ffn(h, w_ffn_gate.astype(BF16), w_ffn_up.astype(BF16), w_ffn_down.astype(BF16), row(ln2_g), row(ln2_b),
                alpha, tm)


def kernel(x, positions, w_in, mu_shift, w_decay_up, w_decay_base, w_aaa_up, w_aaa_base, w_gate_up, k_k, k_a, r_k,
           lnx_g, lnx_b, q_norm_g, w_uq, kv_norm_g, w_ukv, w_proj_rwkv, w_proj_mla, w_out, ln1_g, ln1_b,
           w_ffn_gate, w_ffn_up, w_ffn_down, ln2_g, ln2_b):
    batch, seq, D = x.shape
    depth = w_in.shape[0]
    alpha = (2.0 * depth) ** 0.25
    h = x.reshape(batch * seq, D)
    pos2 = positions.reshape(batch * seq, 1)
    per_layer = (w_in, mu_shift, w_decay_up, w_decay_base, w_aaa_up, w_aaa_base, w_gate_up, k_k, k_a, r_k, lnx_g,
                 lnx_b, q_norm_g, w_uq, kv_norm_g, w_ukv, w_proj_rwkv, w_proj_mla, w_out, ln1_g, ln1_b, w_ffn_gate,
                 w_ffn_up, w_ffn_down, ln2_g, ln2_b)
    for l in range(depth):
        h = _layer(h, pos2, batch, seq, alpha, *(p[l] for p in per_layer))
    return h.reshape(batch, seq, D)
```

```python
import functools
import math

import jax
import jax.numpy as jnp
from jax import lax
from jax.experimental import pallas as pl
from jax.experimental.pallas import tpu as pltpu

F32 = jnp.float32
BF16 = jnp.bfloat16

HEAD_DIM = 64
ROPE_DIM = 32
HALF_ROPE = ROPE_DIM // 2
ROPE_THETA = 10000.0
GN_EPS = 64e-5
LN_EPS = 1e-5
RMS_EPS = 1e-6
NORM_EPS = 1e-12
LANES = 128
MXU_TILE = 256
PART_ROWS = 256
SCAN_CHUNK = 64
VMEM_LIMIT = 56 * 1024 * 1024
NEG_BIG = -1e30


def _dot(a, b):
    return jnp.dot(a, b, preferred_element_type=F32)


def _dot_nt(a, b):
    return lax.dot_general(a, b, (((1,), (1,)), ((), ())), preferred_element_type=F32)


def _dot_tn(a, b):
    return lax.dot_general(a, b, (((0,), (0,)), ((), ())), preferred_element_type=F32)


def _split3(x):
    hi = x.astype(BF16)
    r1 = x - hi.astype(F32)
    mid = r1.astype(BF16)
    lo = (r1 - mid.astype(F32)).astype(BF16)
    return hi, mid, lo


def _head_sums(x, ones_blk):
    hi = x.astype(BF16)
    lo = (x - hi.astype(F32)).astype(BF16)
    w = ones_blk.shape[0]
    cols = [slice(c, c + w) for c in range(0, x.shape[1], w)]
    return jnp.concatenate([_dot(hi[:, c], ones_blk) + _dot(lo[:, c], ones_blk) for c in cols], axis=1)


def _sum_left(m, x):
    hi, mid, lo = _split3(x)
    return _dot(m, hi) + _dot(m, mid) + _dot(m, lo)


def _const_spec(shape):
    nd = len(shape)
    return pl.BlockSpec(shape, lambda *_: (0,) * nd, pipeline_mode=pl.Buffered(1))


def _params(*sem):
    return pltpu.CompilerParams(dimension_semantics=sem, vmem_limit_bytes=VMEM_LIMIT)


def _rwkv_prep_kernel(x_ref, ws_ref, mu_ref, wdu_ref, wau_ref, wgu_ref, dbase_ref, abase_ref, kk_ref, ka_ref,
                      rk_ref, hsum_ref, ctri_ref,
                      rt_ref, at_ref, bt_ref, kt_ref, bh_ref, kh_ref, v_ref, gc_ref, bonus_ref, g_ref,
                      carry_ref, *, tiles_per_seq, rdim):
    tm = x_ref.shape[0]
    first = (pl.program_id(0) % tiles_per_seq) == 0

    @pl.when(first)
    def _():
        carry_ref[...] = jnp.zeros_like(carry_ref)

    th = min(tm, PART_ROWS)
    parts = [slice(i, i + th) for i in range(0, tm, th)]
    xb = [x_ref[p, :].astype(BF16) for p in parts]
    row = lax.broadcasted_iota(jnp.int32, (th, 1), 0)

    def shifted(lo, hi):
        z = [_dot(t, ws_ref[:, lo:hi]) for t in xb]
        prev_last = [carry_ref[0:1, lo:hi]] + [t[th - 1:th, :] for t in z[:-1]]
        carry_ref[0:1, lo:hi] = z[-1][th - 1:th, :]
        out = []
        for zi, prev in zip(z, prev_last):
            z_prev = jnp.where(row == 0, prev, pltpu.roll(zi, 1, axis=0))
            out.append(zi + mu_ref[:, lo:hi] * (z_prev - zi))
        return out

    z_low = shifted(3 * rdim, 3 * rdim + 2 * LANES)
    z_lora = [t[:, 0:LANES] for t in z_low]
    z_gd = [t[:, LANES:2 * LANES] for t in z_low]
    z_k = shifted(rdim, 2 * rdim)
    d_up = [_dot(jnp.tanh(t).astype(BF16), wdu_ref[...]) for t in z_lora]
    a_up = [_dot(t.astype(BF16), wau_ref[...]) for t in z_lora]
    g_up = [_dot(jax.nn.sigmoid(t).astype(BF16), wgu_ref[...]) for t in z_gd]
    z_r = shifted(0, rdim)
    hsum = hsum_ref[...]
    kk_raw = [t * kk_ref[...] for t in z_k]
    kk_ss = [_head_sums(t * t, hsum) for t in kk_raw]
    z_v = shifted(2 * rdim, 3 * rdim)
    a = [jax.nn.sigmoid(abase_ref[...] + t) for t in a_up]
    kmod = [zk * (1.0 + (ai - 1.0) * ka_ref[...]) for zk, ai in zip(z_k, a)]
    bonus = [_head_sums(zr * km * rk_ref[...], hsum) for zr, km in zip(z_r, kmod)]

    ctri = ctri_ref[...]
    C = ctri.shape[0]
    for i, p in enumerate(parts):
        t = -(dbase_ref[...] + d_up[i])
        softplus = jnp.maximum(t, 0.0) + jnp.log(1.0 + jnp.exp(-jnp.abs(t)))
        logw = -jnp.exp(-softplus - 0.5)
        kk = kk_raw[i] * lax.rsqrt(jnp.maximum(kk_ss[i], NORM_EPS * NORM_EPS))
        b = kk * a[i]
        g_ref[p, :] = g_up[i]
        bonus_ref[p, :] = bonus[i] * z_v[i]
        v_ref[p, :] = z_v[i].astype(BF16)
        for c in range(th // C):
            sl = slice(c * C, (c + 1) * C)
            out = slice(p.start + c * C, p.start + (c + 1) * C)
            lw = logw[sl, :]
            cum = _sum_left(ctri, lw)
            e_neg = jnp.exp(-cum)
            chunk_decay = jnp.exp(cum[C - 1:C, :])
            gc_ref[out.start // C:out.start // C + 1, :] = chunk_decay
            b_neg = b[sl, :] * e_neg
            k_neg = kmod[i][sl, :] * e_neg
            rt_ref[out, :] = (z_r[i][sl, :] * jnp.exp(cum)).astype(BF16)
            at_ref[out, :] = (-kk[sl, :] * jnp.exp(cum - lw)).astype(BF16)
            bt_ref[out, :] = b_neg.astype(BF16)
            kt_ref[out, :] = k_neg.astype(BF16)
            bh_ref[out, :] = (b_neg * chunk_decay).astype(BF16)
            kh_ref[out, :] = (k_neg * chunk_decay).astype(BF16)


def _rwkv_prep(x2, ws, mu, wdu, wau, wgu, dbase, abase, k_k, k_a, r_k, hsum, seq, tm):
    T, D = x2.shape
    rdim = k_k.shape[-1]
    ncol = ws.shape[1]
    cpt = tm // SCAN_CHUNK
    r = jnp.arange(SCAN_CHUNK)
    ctri = (r[None, :] <= r[:, None]).astype(BF16)

    row_spec = lambda w: pl.BlockSpec((tm, w), lambda i: (i, 0))
    slab = jax.ShapeDtypeStruct((T, rdim), BF16)
    slab32 = jax.ShapeDtypeStruct((T, rdim), F32)
    out_shape = (slab,) * 7 + (jax.ShapeDtypeStruct((T // SCAN_CHUNK, rdim), F32), slab32, slab32)
    out_specs = (row_spec(rdim),) * 7 + (pl.BlockSpec((cpt, rdim), lambda i: (i, 0)), row_spec(rdim), row_spec(rdim))
    consts = (ws, mu, wdu, wau, wgu, dbase, abase, k_k, k_a, r_k, hsum, ctri)
    return pl.pallas_call(
        functools.partial(_rwkv_prep_kernel, tiles_per_seq=seq // tm, rdim=rdim),
        grid=(T // tm,),
        in_specs=[row_spec(D)] + [_const_spec(c.shape) for c in consts],
        out_specs=out_specs,
        out_shape=out_shape,
        scratch_shapes=[pltpu.VMEM((8, ncol), F32)],
        compiler_params=_params("arbitrary"),
        name="rwkv_prep",
    )(x2, *consts)


def _scan_kernel(rt_ref, at_ref, bt_ref, kt_ref, bh_ref, kh_ref, v_ref, gc_ref, y_ref, s_ref):
    C = SCAN_CHUNK
    nb = rt_ref.shape[0]
    npairs = rt_ref.shape[2] // LANES
    chains = [(g, p) for g in range(nb) for p in range(npairs)]

    @pl.when(pl.program_id(1) == 0)
    def _():
        s_ref[...] = jnp.zeros_like(s_ref)

    row = lax.broadcasted_iota(jnp.int32, (2 * C, 1), 0)
    lane = lax.broadcasted_iota(jnp.int32, (1, LANES), 1)
    keep = (row < C) == (lane < HEAD_DIM)
    t_idx = row % C
    s_idx = lane % C
    m_strict = keep & (s_idx < t_idx)
    m_incl = keep & (s_idx <= t_idx)
    eye = jnp.where(keep & (s_idx == t_idx), 1.0, 0.0)
    keep_b = jnp.where(keep, 1.0, 0.0).astype(BF16)

    def stacked(ref, g, p):
        x = ref[g, :, p * LANES:(p + 1) * LANES]
        return jnp.concatenate([x, x], axis=0) * keep_b

    gram = [_dot_nt(jnp.concatenate([stacked(at_ref, g, p), stacked(rt_ref, g, p)], axis=0),
                    jnp.concatenate([stacked(bt_ref, g, p), stacked(kt_ref, g, p)], axis=0)) for g, p in chains]
    a_ab = [jnp.where(m_strict, gm[0:2 * C, 0:LANES], 0.0) for gm in gram]
    a_ak = [jnp.where(m_strict, gm[0:2 * C, LANES:2 * LANES], 0.0).astype(BF16) for gm in gram]
    a_r = [jnp.concatenate([jnp.where(m_incl, gm[2 * C:4 * C, 0:LANES], 0.0),
                            jnp.where(m_incl, gm[2 * C:4 * C, LANES:2 * LANES], 0.0)], axis=1).astype(BF16)
           for gm in gram]

    inv = [eye + a for a in a_ab]
    pw = [a.astype(BF16) for a in a_ab]
    for _ in range(int(math.log2(C)) - 1):
        pw = [_dot(q, q).astype(BF16) for q in pw]
        inv = [x + _dot(x.astype(BF16), q) for x, q in zip(inv, pw)]

    state = [s_ref[i] for i in range(len(chains))]
    state_b = [s.astype(BF16) for s in state]
    rhs = [_dot_nt(stacked(at_ref, g, p), sb) + _dot(ak, stacked(v_ref, g, p))
           for (g, p), sb, ak in zip(chains, state_b, a_ak)]
    u_b = [_dot(x.astype(BF16), r.astype(BF16)).astype(BF16) for x, r in zip(inv, rhs)]
    uv = [jnp.concatenate([u, stacked(v_ref, g, p)], axis=0) for (g, p), u in zip(chains, u_b)]
    y = [_dot_nt(stacked(rt_ref, g, p), sb) + _dot(ar, w) for (g, p), sb, ar, w in zip(chains, state_b, a_r, uv)]
    new_state = [_dot_tn(w, jnp.concatenate([stacked(bh_ref, g, p), stacked(kh_ref, g, p)], axis=0))
                 for (g, p), w in zip(chains, uv)]
    for i, (g, p) in enumerate(chains):
        sl = slice(p * LANES, (p + 1) * LANES)
        y_ref[g, :, sl] = y[i][0:C, :] + y[i][C:2 * C, :]
        s_ref[i] = state[i] * gc_ref[g, 0, :, sl] + new_state[i]


def _rwkv_scan(rt, at, bt, kt, bh, kh, v, gc, batch, seq, nb):
    T, rdim = rt.shape
    C = SCAN_CHUNK
    nchunk = seq // C
    view = lambda t: t.reshape(batch, seq, rdim)
    slab_spec = pl.BlockSpec((nb, C, rdim), lambda b, c: (b, c, 0))
    y = pl.pallas_call(
        _scan_kernel,
        grid=(batch // nb, nchunk),
        in_specs=[slab_spec] * 7 + [pl.BlockSpec((nb, 1, 1, rdim), lambda b, c: (b, c, 0, 0))],
        out_specs=slab_spec,
        out_shape=jax.ShapeDtypeStruct((batch, seq, rdim), F32),
        scratch_shapes=[pltpu.VMEM((nb * rdim // LANES, LANES, LANES), F32)],
        compiler_params=_params("arbitrary", "arbitrary"),
        name="rwkv_scan",
    )(*(view(t) for t in (rt, at, bt, kt, bh, kh, v)), gc.reshape(batch, nchunk, 1, rdim))
    return y.reshape(T, rdim)


def _mla_prep_kernel(x_ref, pos_ref, wm_ref, qg_ref, kvg_ref, wuq_ref, wuk_ref, wuv_ref, freq_ref, cplace_ref,
                     splace_ref, nope_ref, q_ref, k_ref, v_ref, *, q_lora, kv_lora, heads):
    tm = x_ref.shape[0]
    parts = [slice(i * (tm // 2), (i + 1) * (tm // 2)) for i in range(2)]

    def rms(t, gain):
        return t * lax.rsqrt(jnp.mean(t * t, axis=-1, keepdims=True) + RMS_EPS) * gain

    z = [_dot(x_ref[p, :].astype(BF16), wm_ref[...]) for p in parts]
    qn = [rms(t[:, 0:q_lora], qg_ref[...]).astype(BF16) for t in z]
    kvn = [rms(t[:, q_lora:q_lora + kv_lora], kvg_ref[...]).astype(BF16) for t in z]
    q = [_dot(t, wuq_ref[...]) for t in qn]
    kn = [_dot(t, wuk_ref[...]) for t in kvn]
    vt = [_dot_nt(wuv_ref[...], t) for t in kvn]

    q_scale = (HEAD_DIM + ROPE_DIM) ** -0.5 * math.log2(math.e)
    for i, p in enumerate(parts):
        v_ref[0, :, p] = vt[i].astype(BF16)
        ang_t = freq_ref[...] * pos_ref[0, :, p].astype(F32)
        cs = sum(_dot_tn(t, cplace_ref[...]) for t in _split3(jnp.cos(ang_t))) + nope_ref[...]
        sn = sum(_dot_tn(t, splace_ref[...]) for t in _split3(jnp.sin(ang_t)))

        def rope(t):
            return t * cs + pltpu.roll(t, LANES - HALF_ROPE, axis=1) * sn

        kpe_rot = rope(z[i][:, q_lora + kv_lora:q_lora + kv_lora + LANES])
        for h in range(heads):
            sl = slice(h * LANES, (h + 1) * LANES)
            q_ref[p, sl] = (rope(q[i][:, sl]) * q_scale).astype(BF16)
            k_ref[p, sl] = (kn[i][:, sl] + kpe_rot).astype(BF16)


def _mla_prep(x2, pos3, wm, qg, kvg, wuq, wuk, wuv, freq, cplace, splace, nope, heads, tm):
    T, D = x2.shape
    q_lora, kv_lora = qg.shape[-1], kvg.shape[-1]
    row_spec = lambda w: pl.BlockSpec((tm, w), lambda i: (i, 0))
    consts = (wm, qg, kvg, wuq, wuk, wuv, freq, cplace, splace, nope)
    return pl.pallas_call(
        functools.partial(_mla_prep_kernel, q_lora=q_lora, kv_lora=kv_lora, heads=heads),
        grid=(T // tm,),
        in_specs=[row_spec(D), pl.BlockSpec((1, 1, tm), lambda i: (i, 0, 0))] + [_const_spec(c.shape) for c in consts],
        out_specs=(row_spec(heads * LANES), row_spec(heads * LANES),
                   pl.BlockSpec((1, heads * HEAD_DIM, tm), lambda i: (i, 0, 0))),
        out_shape=(jax.ShapeDtypeStruct((T, heads * LANES), BF16), jax.ShapeDtypeStruct((T, heads * LANES), BF16),
                   jax.ShapeDtypeStruct((T // tm, heads * HEAD_DIM, tm), BF16)),
        compiler_params=_params("arbitrary"),
        name="mla_prep",
    )(x2, pos3, *consts)


def _attn_kernel(q_ref, k_ref, vt_ref, o_ref, m_ref, l_ref, acc_ref, sa_ref, sb_ref):
    tk = vt_ref.shape[2]
    tq = tk
    nq = q_ref.shape[0] // tq
    nh = q_ref.shape[1] // LANES
    key_idx = lax.broadcasted_iota(jnp.int32, (tk, 1), 0)
    qry_idx = lax.broadcasted_iota(jnp.int32, (1, tq), 1)

    def reset():
        m_ref[...] = jnp.full_like(m_ref, NEG_BIG)
        l_ref[...] = jnp.zeros_like(l_ref)
        acc_ref[...] = jnp.zeros_like(acc_ref)

    def scores(qi, t, s_ref):
        qi = jnp.minimum(qi, nq - 1)
        kblk = k_ref[pl.ds(pl.multiple_of(t * tk, tk), tk), :]
        qblk = q_ref[pl.ds(pl.multiple_of(qi * tq, tq), tq), :]
        for h in range(nh):
            sl = slice(h * LANES, (h + 1) * LANES)
            s_ref[h] = _dot_nt(kblk[:, sl], qblk[:, sl])

    def consume(t, s_ref, visible):
        vt = vt_ref[t]
        for h in range(nh):
            s = s_ref[h] if visible is None else jnp.where(visible, s_ref[h], NEG_BIG)
            m_old = m_ref[h]
            m_new = jnp.maximum(m_old, jnp.max(s, axis=0, keepdims=True))
            p = jnp.exp2(s - m_new)
            alpha = jnp.exp2(m_old - m_new)
            l_ref[h] = alpha * l_ref[h] + jnp.sum(p, axis=0, keepdims=True)
            m_ref[h] = m_new
            acc_ref[h] = alpha * acc_ref[h] + _dot(vt[h * HEAD_DIM:(h + 1) * HEAD_DIM, :], p.astype(BF16))

    def finish(qi):
        out_t = jnp.concatenate([acc_ref[h] / l_ref[h] for h in range(nh)], axis=0)
        o_ref[pl.ds(pl.multiple_of(qi * tq, tq), tq), :] = out_t.T.astype(o_ref.dtype)
        reset()

    def succ(qi, t):
        last = t == qi
        return jnp.where(last, qi + 1, qi), jnp.where(last, 0, t + 1)

    def close(qi, t, s_ref, diagonal):
        consume(t, s_ref, key_idx <= qry_idx if diagonal else None)
        if diagonal:
            finish(qi)

    reset()
    scores(0, 0, sa_ref)

    def body(j, carry):
        qi, t = carry
        q1, t1 = succ(qi, t)
        q2, t2 = succ(q1, t1)
        for d0 in (False, True):
            for d1 in (False, True):
                @pl.when(((t == qi) == d0) & ((t1 == q1) == d1))
                def _(d0=d0, d1=d1):
                    scores(q1, t1, sb_ref)
                    close(qi, t, sa_ref, d0)
                    scores(q2, t2, sa_ref)
                    close(q1, t1, sb_ref, d1)
        return q2, t2

    total = nq * (nq + 1) // 2
    lax.fori_loop(0, total // 2, body, (jnp.int32(0), jnp.int32(0)))
    if total % 2:
        close(jnp.int32(nq - 1), jnp.int32(nq - 1), sa_ref, True)


def _mla_attn(q, k, vt, batch, seq, heads, nh):
    T = q.shape[0]
    tk = vt.shape[2]
    nk = seq // tk
    vt4 = vt.reshape(batch, nk, heads * HEAD_DIM, tk)
    return pl.pallas_call(
        _attn_kernel,
        grid=(batch, heads // nh),
        in_specs=[pl.BlockSpec((seq, nh * LANES), lambda b, p: (b, p)),
                  pl.BlockSpec((seq, nh * LANES), lambda b, p: (b, p)),
                  pl.BlockSpec((None, nk, nh * HEAD_DIM, tk), lambda b, p: (b, 0, p, 0))],
        out_specs=pl.BlockSpec((seq, nh * HEAD_DIM), lambda b, p: (b, p)),
        out_shape=jax.ShapeDtypeStruct((T, heads * HEAD_DIM), BF16),
        scratch_shapes=[pltpu.VMEM((nh, 1, tk), F32), pltpu.VMEM((nh, 1, tk), F32),
                        pltpu.VMEM((nh, HEAD_DIM, tk), F32),
                        pltpu.VMEM((nh, tk, tk), F32), pltpu.VMEM((nh, tk, tk), F32)],
        compiler_params=_params("arbitrary", "arbitrary"),
        name="mla_attn",
    )(q, k, vt4)


def _layer_norm(t, gain, bias):
    mu = jnp.mean(t, axis=-1, keepdims=True)
    c = t - mu
    var = jnp.mean(c * c, axis=-1, keepdims=True)
    return c * lax.rsqrt(var + LN_EPS) * gain + bias


def _mix_kernel(x_ref, y_ref, bonus_ref, g_ref, ym_ref, wg_ref, wpr_ref, wpm_ref, wo_ref, lnxg_ref, lnxb_ref,
                ln1g_ref, ln1b_ref, hsum_ref, h_ref, *, alpha):
    tm, d = x_ref.shape
    parts = [slice(0, tm // 2), slice(tm // 2, tm)]
    hsum = hsum_ref[...]
    inv_n = 1.0 / HEAD_DIM
    gates = [_dot(x_ref[p, :].astype(BF16), wg_ref[...]) for p in parts]
    pm = [_dot(ym_ref[p, :], wpm_ref[...]) for p in parts]
    mean = [_head_sums(y_ref[p, :], hsum) * inv_n for p in parts]
    yc = [y_ref[p, :] - mu for p, mu in zip(parts, mean)]
    var = [_head_sums(c * c, hsum) * inv_n for c in yc]
    y_r = [(((c * lax.rsqrt(v + GN_EPS) * lnxg_ref[...] + lnxb_ref[...]) + bonus_ref[p, :]) * g_ref[p, :]).astype(BF16)
           for p, c, v in zip(parts, yc, var)]
    pr = [_dot(t, wpr_ref[...]) for t in y_r]
    merged = [(jax.nn.sigmoid(gt[:, 0:d]) * a + jax.nn.sigmoid(gt[:, d:2 * d]) * b).astype(BF16)
              for gt, a, b in zip(gates, pr, pm)]
    out = [_dot(t, wo_ref[...]) for t in merged]
    for p, o in zip(parts, out):
        h_ref[p, :] = _layer_norm(alpha * x_ref[p, :] + o, ln1g_ref[...], ln1b_ref[...])


def _mix(x2, y, bonus, g, ym, wg, wpr, wpm, wo, lnxg, lnxb, ln1g, ln1b, hsum, alpha, tm):
    T, D = x2.shape
    rdim = y.shape[1]
    row_spec = lambda w: pl.BlockSpec((tm, w), lambda i: (i, 0))
    consts = (wg, wpr, wpm, wo, lnxg, lnxb, ln1g, ln1b, hsum)
    return pl.pallas_call(
        functools.partial(_mix_kernel, alpha=alpha),
        grid=(T // tm,),
        in_specs=[row_spec(D), row_spec(rdim), row_spec(rdim), row_spec(rdim), row_spec(ym.shape[1])]
        + [_const_spec(c.shape) for c in consts],
        out_specs=row_spec(D),
        out_shape=jax.ShapeDtypeStruct((T, D), F32),
        compiler_params=_params("arbitrary"),
        name="mix",
    )(x2, y, bonus, g, ym, *consts)


def _ffn_kernel(h_ref, wg_ref, wu_ref, wd_ref, ln2g_ref, ln2b_ref, o_ref, *, alpha, ff_bounds):
    h = h_ref[...]
    hb = h.astype(BF16)
    acc = alpha * h
    chunks = list(zip(ff_bounds[:-1], ff_bounds[1:]))
    for lo, hi in chunks[:-1]:
        sl = slice(lo, hi)
        gate = _dot(hb, wg_ref[:, sl])
        act = (gate * jax.nn.sigmoid(gate) * _dot(hb, wu_ref[:, sl])).astype(BF16)
        acc = acc + _dot(act, wd_ref[sl, :])
    sl = slice(*chunks[-1])
    gate = _dot(hb, wg_ref[:, sl])
    act = (gate * jax.nn.sigmoid(gate) * _dot(hb, wu_ref[:, sl])).astype(BF16)
    tm = h.shape[0]
    th = min(tm, PART_ROWS)
    for r in range(0, tm, th):
        rows = slice(r, r + th)
        o_ref[rows, :] = _layer_norm(acc[rows, :] + _dot(act[rows, :], wd_ref[sl, :]), ln2g_ref[...], ln2b_ref[...])


def _ffn(h, wg, wu, wd, ln2g, ln2b, alpha, tm):
    T, D = h.shape
    dff = wg.shape[1]
    split = min(dff, -(-(dff // 2) // MXU_TILE) * MXU_TILE)
    ff_bounds = (0, split, dff) if split < dff else (0, dff)
    row_spec = pl.BlockSpec((tm, D), lambda i: (i, 0))
    consts = (wg, wu, wd, ln2g, ln2b)
    return pl.pallas_call(
        functools.partial(_ffn_kernel, alpha=alpha, ff_bounds=ff_bounds),
        grid=(T // tm,),
        in_specs=[row_spec] + [_const_spec(c.shape) for c in consts],
        out_specs=row_spec,
        out_shape=jax.ShapeDtypeStruct((T, D), F32),
        compiler_params=_params("arbitrary"),
        name="ffn",
    )(h, *consts)


def _layer(x2, pos2, batch, seq, alpha, w_in, mu_shift, w_decay_up, w_decay_base, w_aaa_up, w_aaa_base, w_gate_up,
           k_k, k_a, r_k, lnx_g, lnx_b, q_norm_g, w_uq, kv_norm_g, w_ukv, w_proj_rwkv, w_proj_mla, w_out,
           ln1_g, ln1_b, w_ffn_gate, w_ffn_up, w_ffn_down, ln2_g, ln2_b):
    D = x2.shape[1]
    rdim = k_k.shape[0]
    dl, al, gl = w_decay_up.shape[0], w_aaa_up.shape[0], w_gate_up.shape[0]
    q_lora, kv_lora = q_norm_g.shape[0], kv_norm_g.shape[0]
    heads = w_uq.shape[1] // (HEAD_DIM + ROPE_DIM)
    assert rdim % LANES == 0 and dl + al == LANES and gl == LANES and heads % 2 == 0
    assert w_ukv.shape[1] == heads * 2 * HEAD_DIM and w_proj_mla.shape[0] == heads * HEAD_DIM
    shift_cols = 3 * rdim + dl + al + gl
    tm = min(512, seq)
    assert seq % tm == 0 and tm % SCAN_CHUNK == 0
    row = lambda t: t.reshape(1, -1).astype(F32)

    ws = w_in[:, :shift_cols].astype(BF16)
    wdu = jnp.concatenate([w_decay_up, jnp.zeros((al, rdim), F32)], axis=0).astype(BF16)
    wau = jnp.concatenate([jnp.zeros((dl, rdim), F32), w_aaa_up], axis=0).astype(BF16)
    assert rdim % MXU_TILE == 0
    hid = jnp.arange(MXU_TILE) // HEAD_DIM
    hsum = (hid[:, None] == hid[None, :]).astype(BF16)
    rt, at, bt, kt, bh, kh, v, gc, bonus, g = _rwkv_prep(
        x2, ws, row(mu_shift), wdu, wau, w_gate_up.astype(BF16), row(w_decay_base), row(w_aaa_base), row(k_k),
        row(k_a), row(r_k), hsum, seq, tm)
    y = _rwkv_scan(rt, at, bt, kt, bh, kh, v, gc, batch, seq, 4 if batch % 4 == 0 else 1)

    o = shift_cols
    w_cq, w_ckv = w_in[:, o:o + q_lora], w_in[:, o + q_lora:o + q_lora + kv_lora]
    w_kpe = w_in[:, o + q_lora + kv_lora:o + q_lora + kv_lora + ROPE_DIM]
    pad_r = LANES - HEAD_DIM - ROPE_DIM - HALF_ROPE
    wm = jnp.concatenate([w_cq, w_ckv, jnp.zeros((D, HEAD_DIM), F32), w_kpe, w_kpe[:, :HALF_ROPE],
                          jnp.zeros((D, pad_r), F32)], axis=1).astype(BF16)
    uq = w_uq.reshape(q_lora, heads, HEAD_DIM + ROPE_DIM)
    wuq = jnp.concatenate([uq, uq[:, :, HEAD_DIM:HEAD_DIM + HALF_ROPE], jnp.zeros((q_lora, heads, pad_r), F32)],
                          axis=2).reshape(q_lora, heads * LANES).astype(BF16)
    ukv = w_ukv.reshape(kv_lora, heads, 2 * HEAD_DIM)
    wuk = jnp.pad(ukv[:, :, :HEAD_DIM], ((0, 0), (0, 0), (0, LANES - HEAD_DIM))).reshape(kv_lora, heads * LANES)
    wuv = ukv[:, :, HEAD_DIM:].reshape(kv_lora, heads * HEAD_DIM).T
    inv_freq = ROPE_THETA ** (-jnp.arange(0, ROPE_DIM, 2, dtype=F32) / ROPE_DIM)
    lane = jnp.arange(LANES)[None, :]
    j = jnp.arange(HALF_ROPE)[:, None]
    on_x1, on_x2 = lane == HEAD_DIM + j, lane == HEAD_DIM + HALF_ROPE + j
    cplace = (on_x1 | on_x2).astype(BF16)
    splace = (on_x2.astype(F32) - on_x1.astype(F32)).astype(BF16)
    nope = (lane < HEAD_DIM).astype(F32)
    q, k, vv = _mla_prep(x2, pos2.reshape(-1, 1, tm), wm, row(q_norm_g), row(kv_norm_g), wuq, wuk.astype(BF16),
                         wuv.astype(BF16), inv_freq.reshape(HALF_ROPE, 1), cplace, splace, nope, heads, tm)
    ym = _mla_attn(q, k, vv, batch, seq, heads, 4 if heads % 4 == 0 else 2)

    o2 = o + q_lora + kv_lora + ROPE_DIM
    wg = w_in[:, o2:o2 + 2 * D].astype(BF16)
    h = _mix(x2, y, bonus, g, ym, wg, w_proj_rwkv.astype(BF16), w_proj_mla.astype(BF16), w_out.astype(BF16),
             row(lnx_g), row(lnx_b), row(ln1_g), row(ln1_b), hsum, alpha, tm)
    return _ffn(h, w_ffn_gate.astype(BF16), w_ffn_up.astype(BF16), w_ffn_down.astype(BF16), row(ln2_g), row(ln2_b),
                alpha, tm)


def kernel(x, positions, w_in, mu_shift, w_decay_up, w_decay_base, w_aaa_up, w_aaa_base, w_gate_up, k_k, k_a, r_k,
           lnx_g, lnx_b, q_norm_g, w_uq, kv_norm_g, w_ukv, w_proj_rwkv, w_proj_mla, w_out, ln1_g, ln1_b,
           w_ffn_gate, w_ffn_up, w_ffn_down, ln2_g, ln2_b):
    batch, seq, D = x.shape
    depth = w_in.shape[0]
    alpha = (2.0 * depth) ** 0.25
    h = x.reshape(batch * seq, D)
    pos2 = positions.reshape(batch * seq, 1)
    per_layer = (w_in, mu_shift, w_decay_up, w_decay_base, w_aaa_up, w_aaa_base, w_gate_up, k_k, k_a, r_k, lnx_g,
                 lnx_b, q_norm_g, w_uq, kv_norm_g, w_ukv, w_proj_rwkv, w_proj_mla, w_out, ln1_g, ln1_b, w_ffn_gate,
                 w_ffn_up, w_ffn_down, ln2_g, ln2_b)
    for l in range(depth):
        layer_params = (p.reshape(p.shape[1:]) if depth == 1 else p[l] for p in per_layer)
        h = _layer(h, pos2, batch, seq, alpha, *layer_params)
    return h.reshape(batch, seq, D)
```

```python
import functools
import math

import jax
import jax.numpy as jnp
from jax import lax
from jax.experimental import pallas as pl
from jax.experimental.pallas import tpu as pltpu

F32 = jnp.float32
BF16 = jnp.bfloat16

HEAD_DIM = 64
ROPE_DIM = 32
HALF_ROPE = ROPE_DIM // 2
ROPE_THETA = 10000.0
GN_EPS = 64e-5
LN_EPS = 1e-5
RMS_EPS = 1e-6
NORM_EPS = 1e-12
LANES = 128
MXU_TILE = 256
PART_ROWS = 256
SCAN_CHUNK = 64
VMEM_LIMIT = 56 * 1024 * 1024
NEG_BIG = -1e30


def _dot(a, b):
    return jnp.dot(a, b, preferred_element_type=F32)


def _dot_nt(a, b):
    return lax.dot_general(a, b, (((1,), (1,)), ((), ())), preferred_element_type=F32)


def _dot_tn(a, b):
    return lax.dot_general(a, b, (((0,), (0,)), ((), ())), preferred_element_type=F32)


def _split3(x):
    hi = x.astype(BF16)
    r1 = x - hi.astype(F32)
    mid = r1.astype(BF16)
    lo = (r1 - mid.astype(F32)).astype(BF16)
    return hi, mid, lo


def _head_sums(x, ones_blk):
    hi = x.astype(BF16)
    lo = (x - hi.astype(F32)).astype(BF16)
    w = ones_blk.shape[0]
    cols = [slice(c, c + w) for c in range(0, x.shape[1], w)]
    return jnp.concatenate([_dot(hi[:, c], ones_blk) + _dot(lo[:, c], ones_blk) for c in cols], axis=1)


def _sum_left(m, x):
    hi, mid, lo = _split3(x)
    return _dot(m, hi) + _dot(m, mid) + _dot(m, lo)


def _const_spec(shape):
    nd = len(shape)
    return pl.BlockSpec(shape, lambda *_: (0,) * nd, pipeline_mode=pl.Buffered(1))


def _params(*sem):
    return pltpu.CompilerParams(dimension_semantics=sem, vmem_limit_bytes=VMEM_LIMIT)


def _rwkv_prep_kernel(x_ref, ws_ref, mu_ref, wdu_ref, wau_ref, wgu_ref, dbase_ref, abase_ref, kk_ref, ka_ref,
                      rk_ref, hsum_ref, ctri_ref,
                      rt_ref, at_ref, bt_ref, kt_ref, bh_ref, kh_ref, v_ref, gc_ref, bonus_ref, g_ref,
                      carry_ref, *, tiles_per_seq, rdim):
    tm = x_ref.shape[0]
    first = (pl.program_id(0) % tiles_per_seq) == 0

    @pl.when(first)
    def _():
        carry_ref[...] = jnp.zeros_like(carry_ref)

    th = min(tm, PART_ROWS)
    parts = [slice(i, i + th) for i in range(0, tm, th)]
    xb = [x_ref[p, :].astype(BF16) for p in parts]
    row = lax.broadcasted_iota(jnp.int32, (th, 1), 0)

    def shifted(lo, hi):
        z = [_dot(t, ws_ref[:, lo:hi]) for t in xb]
        prev_last = [carry_ref[0:1, lo:hi]] + [t[th - 1:th, :] for t in z[:-1]]
        carry_ref[0:1, lo:hi] = z[-1][th - 1:th, :]
        out = []
        for zi, prev in zip(z, prev_last):
            z_prev = jnp.where(row == 0, prev, pltpu.roll(zi, 1, axis=0))
            out.append(zi + mu_ref[:, lo:hi] * (z_prev - zi))
        return out

    z_low = shifted(3 * rdim, 3 * rdim + 2 * LANES)
    z_lora = [t[:, 0:LANES] for t in z_low]
    z_gd = [t[:, LANES:2 * LANES] for t in z_low]
    z_k = shifted(rdim, 2 * rdim)
    d_up = [_dot(jnp.tanh(t).astype(BF16), wdu_ref[...]) for t in z_lora]
    a_up = [_dot(t.astype(BF16), wau_ref[...]) for t in z_lora]
    g_up = [_dot(jax.nn.sigmoid(t).astype(BF16), wgu_ref[...]) for t in z_gd]
    z_r = shifted(0, rdim)
    hsum = hsum_ref[...]
    kk_raw = [t * kk_ref[...] for t in z_k]
    kk_ss = [_head_sums(t * t, hsum) for t in kk_raw]
    z_v = shifted(2 * rdim, 3 * rdim)
    a = [jax.nn.sigmoid(abase_ref[...] + t) for t in a_up]
    kmod = [zk * (1.0 + (ai - 1.0) * ka_ref[...]) for zk, ai in zip(z_k, a)]
    bonus = [_head_sums(zr * km * rk_ref[...], hsum) for zr, km in zip(z_r, kmod)]

    ctri = ctri_ref[...]
    C = ctri.shape[0]
    for i, p in enumerate(parts):
        t = -(dbase_ref[...] + d_up[i])
        softplus = jnp.maximum(t, 0.0) + jnp.log(1.0 + jnp.exp(-jnp.abs(t)))
        logw = -jnp.exp(-softplus - 0.5)
        kk = kk_raw[i] * lax.rsqrt(jnp.maximum(kk_ss[i], NORM_EPS * NORM_EPS))
        b = kk * a[i]
        g_ref[p, :] = g_up[i]
        bonus_ref[p, :] = bonus[i] * z_v[i]
        v_ref[p, :] = z_v[i].astype(BF16)
        for c in range(th // C):
            sl = slice(c * C, (c + 1) * C)
            out = slice(p.start + c * C, p.start + (c + 1) * C)
            lw = logw[sl, :]
            cum = _sum_left(ctri, lw)
            e_neg = jnp.exp(-cum)
            chunk_decay = jnp.exp(cum[C - 1:C, :])
            gc_ref[out.start // C:out.start // C + 1, :] = chunk_decay
            b_neg = b[sl, :] * e_neg
            k_neg = kmod[i][sl, :] * e_neg
            rt_ref[out, :] = (z_r[i][sl, :] * jnp.exp(cum)).astype(BF16)
            at_ref[out, :] = (-kk[sl, :] * jnp.exp(cum - lw)).astype(BF16)
            bt_ref[out, :] = b_neg.astype(BF16)
            kt_ref[out, :] = k_neg.astype(BF16)
            bh_ref[out, :] = (b_neg * chunk_decay).astype(BF16)
            kh_ref[out, :] = (k_neg * chunk_decay).astype(BF16)


def _rwkv_prep(x2, ws, mu, wdu, wau, wgu, dbase, abase, k_k, k_a, r_k, hsum, seq, tm):
    T, D = x2.shape
    rdim = k_k.shape[-1]
    ncol = ws.shape[1]
    cpt = tm // SCAN_CHUNK
    r = jnp.arange(SCAN_CHUNK)
    ctri = (r[None, :] <= r[:, None]).astype(BF16)

    row_spec = lambda w: pl.BlockSpec((tm, w), lambda i: (i, 0))
    slab = jax.ShapeDtypeStruct((T, rdim), BF16)
    slab32 = jax.ShapeDtypeStruct((T, rdim), F32)
    out_shape = (slab,) * 7 + (jax.ShapeDtypeStruct((T // SCAN_CHUNK, rdim), F32), slab32, slab32)
    out_specs = (row_spec(rdim),) * 7 + (pl.BlockSpec((cpt, rdim), lambda i: (i, 0)), row_spec(rdim), row_spec(rdim))
    consts = (ws, mu, wdu, wau, wgu, dbase, abase, k_k, k_a, r_k, hsum, ctri)
    return pl.pallas_call(
        functools.partial(_rwkv_prep_kernel, tiles_per_seq=seq // tm, rdim=rdim),
        grid=(T // tm,),
        in_specs=[row_spec(D)] + [_const_spec(c.shape) for c in consts],
        out_specs=out_specs,
        out_shape=out_shape,
        scratch_shapes=[pltpu.VMEM((8, ncol), F32)],
        compiler_params=_params("arbitrary"),
        name="rwkv_prep",
    )(x2, *consts)


def _scan_kernel(rt_ref, at_ref, bt_ref, kt_ref, bh_ref, kh_ref, v_ref, gc_ref, y_ref, s_ref):
    C = SCAN_CHUNK
    nb = rt_ref.shape[0]
    npairs = rt_ref.shape[2] // LANES
    chains = [(g, p) for g in range(nb) for p in range(npairs)]

    @pl.when(pl.program_id(1) == 0)
    def _():
        s_ref[...] = jnp.zeros_like(s_ref)

    row = lax.broadcasted_iota(jnp.int32, (2 * C, 1), 0)
    lane = lax.broadcasted_iota(jnp.int32, (1, LANES), 1)
    keep = (row < C) == (lane < HEAD_DIM)
    t_idx = row % C
    s_idx = lane % C
    m_strict = keep & (s_idx < t_idx)
    m_incl = keep & (s_idx <= t_idx)
    eye = jnp.where(keep & (s_idx == t_idx), 1.0, 0.0)
    keep_b = jnp.where(keep, 1.0, 0.0).astype(BF16)

    def stacked(ref, g, p):
        x = ref[g, :, p * LANES:(p + 1) * LANES]
        return jnp.concatenate([x, x], axis=0) * keep_b

    gram = [_dot_nt(jnp.concatenate([stacked(at_ref, g, p), stacked(rt_ref, g, p)], axis=0),
                    jnp.concatenate([stacked(bt_ref, g, p), stacked(kt_ref, g, p)], axis=0)) for g, p in chains]
    a_ab = [jnp.where(m_strict, gm[0:2 * C, 0:LANES], 0.0) for gm in gram]
    a_ak = [jnp.where(m_strict, gm[0:2 * C, LANES:2 * LANES], 0.0).astype(BF16) for gm in gram]
    a_r = [jnp.concatenate([jnp.where(m_incl, gm[2 * C:4 * C, 0:LANES], 0.0),
                            jnp.where(m_incl, gm[2 * C:4 * C, LANES:2 * LANES], 0.0)], axis=1).astype(BF16)
           for gm in gram]

    inv = [eye + a for a in a_ab]
    pw = [a.astype(BF16) for a in a_ab]
    for _ in range(int(math.log2(C)) - 1):
        pw = [_dot(q, q).astype(BF16) for q in pw]
        inv = [x + _dot(x.astype(BF16), q) for x, q in zip(inv, pw)]

    state = [s_ref[i] for i in range(len(chains))]
    state_b = [s.astype(BF16) for s in state]
    rhs = [_dot_nt(stacked(at_ref, g, p), sb) + _dot(ak, stacked(v_ref, g, p))
           for (g, p), sb, ak in zip(chains, state_b, a_ak)]
    u_b = [_dot(x.astype(BF16), r.astype(BF16)).astype(BF16) for x, r in zip(inv, rhs)]
    uv = [jnp.concatenate([u, stacked(v_ref, g, p)], axis=0) for (g, p), u in zip(chains, u_b)]
    y = [_dot_nt(stacked(rt_ref, g, p), sb) + _dot(ar, w) for (g, p), sb, ar, w in zip(chains, state_b, a_r, uv)]
    new_state = [_dot_tn(w, jnp.concatenate([stacked(bh_ref, g, p), stacked(kh_ref, g, p)], axis=0))
                 for (g, p), w in zip(chains, uv)]
    keep_f = jnp.where(keep, 1.0, 0.0)
    inv_n = 1.0 / HEAD_DIM
    for i, (g, p) in enumerate(chains):
        sl = slice(p * LANES, (p + 1) * LANES)
        mean = jnp.sum(y[i], axis=1, keepdims=True) * inv_n
        yc = (y[i] - mean) * keep_f
        var = jnp.sum(yc * yc, axis=1, keepdims=True) * inv_n
        yn = yc * lax.rsqrt(var + GN_EPS)
        y_ref[g, :, sl] = yn[0:C, :] + yn[C:2 * C, :]
        s_ref[i] = state[i] * gc_ref[g, 0, :, sl] + new_state[i]


def _rwkv_scan(rt, at, bt, kt, bh, kh, v, gc, batch, seq, nb):
    T, rdim = rt.shape
    C = SCAN_CHUNK
    nchunk = seq // C
    view = lambda t: t.reshape(batch, seq, rdim)
    slab_spec = pl.BlockSpec((nb, C, rdim), lambda b, c: (b, c, 0))
    y = pl.pallas_call(
        _scan_kernel,
        grid=(batch // nb, nchunk),
        in_specs=[slab_spec] * 7 + [pl.BlockSpec((nb, 1, 1, rdim), lambda b, c: (b, c, 0, 0))],
        out_specs=slab_spec,
        out_shape=jax.ShapeDtypeStruct((batch, seq, rdim), F32),
        scratch_shapes=[pltpu.VMEM((nb * rdim // LANES, LANES, LANES), F32)],
        compiler_params=_params("arbitrary", "arbitrary"),
        name="rwkv_scan",
    )(*(view(t) for t in (rt, at, bt, kt, bh, kh, v)), gc.reshape(batch, nchunk, 1, rdim))
    return y.reshape(T, rdim)


def _mla_prep_kernel(x_ref, pos_ref, wm_ref, qg_ref, kvg_ref, wuq_ref, wuk_ref, wuv_ref, freq_ref, cplace_ref,
                     splace_ref, q_ref, k_ref, v_ref, *, q_lora, kv_lora, heads):
    tm = x_ref.shape[0]
    parts = [slice(i * (tm // 2), (i + 1) * (tm // 2)) for i in range(2)]

    def rms(t, gain):
        return t * lax.rsqrt(jnp.mean(t * t, axis=-1, keepdims=True) + RMS_EPS) * gain

    z = [_dot(x_ref[p, :].astype(BF16), wm_ref[...]) for p in parts]
    qn = [rms(t[:, 0:q_lora], qg_ref[...]).astype(BF16) for t in z]
    kvn = [rms(t[:, q_lora:q_lora + kv_lora], kvg_ref[...]).astype(BF16) for t in z]
    qt = [_dot_nt(wuq_ref[...], t) for t in qn]
    kn = [_dot(t, wuk_ref[...]) for t in kvn]
    vt = [_dot_nt(wuv_ref[...], t) for t in kvn]

    q_scale = (HEAD_DIM + ROPE_DIM) ** -0.5 * math.log2(math.e)
    x1, x2 = slice(HEAD_DIM, HEAD_DIM + HALF_ROPE), slice(HEAD_DIM + HALF_ROPE, HEAD_DIM + ROPE_DIM)
    for i, p in enumerate(parts):
        v_ref[0, :, p] = vt[i].astype(BF16)
        ang_t = freq_ref[...] * pos_ref[0, :, p].astype(F32)
        cos_t, sin_t = jnp.cos(ang_t), jnp.sin(ang_t)
        for h in range(heads):
            blk = qt[i][h * LANES:(h + 1) * LANES, :]
            rot = jnp.concatenate([blk[0:HEAD_DIM, :], blk[x1, :] * cos_t - blk[x2, :] * sin_t,
                                   blk[x2, :] * cos_t + blk[x1, :] * sin_t, blk[HEAD_DIM + ROPE_DIM:, :]], axis=0)
            q_ref[0, h * LANES:(h + 1) * LANES, p] = (rot * q_scale).astype(BF16)
        cs = sum(_dot_tn(t, cplace_ref[...]) for t in _split3(cos_t))
        sn = sum(_dot_tn(t, splace_ref[...]) for t in _split3(sin_t))
        kpe = z[i][:, q_lora + kv_lora:q_lora + kv_lora + LANES]
        kpe_rot = kpe * cs + pltpu.roll(kpe, LANES - HALF_ROPE, axis=1) * sn
        for h in range(heads):
            sl = slice(h * LANES, (h + 1) * LANES)
            k_ref[p, sl] = (kn[i][:, sl] + kpe_rot).astype(BF16)


def _mla_prep(x2, pos3, wm, qg, kvg, wuq, wuk, wuv, freq, cplace, splace, heads, tm):
    T, D = x2.shape
    q_lora, kv_lora = qg.shape[-1], kvg.shape[-1]
    row_spec = lambda w: pl.BlockSpec((tm, w), lambda i: (i, 0))
    tile_t = lambda w: pl.BlockSpec((1, w, tm), lambda i: (i, 0, 0))
    consts = (wm, qg, kvg, wuq, wuk, wuv, freq, cplace, splace)
    return pl.pallas_call(
        functools.partial(_mla_prep_kernel, q_lora=q_lora, kv_lora=kv_lora, heads=heads),
        grid=(T // tm,),
        in_specs=[row_spec(D), pl.BlockSpec((1, 1, tm), lambda i: (i, 0, 0))] + [_const_spec(c.shape) for c in consts],
        out_specs=(tile_t(heads * LANES), row_spec(heads * LANES), tile_t(heads * HEAD_DIM)),
        out_shape=(jax.ShapeDtypeStruct((T // tm, heads * LANES, tm), BF16),
                   jax.ShapeDtypeStruct((T, heads * LANES), BF16),
                   jax.ShapeDtypeStruct((T // tm, heads * HEAD_DIM, tm), BF16)),
        compiler_params=_params("arbitrary"),
        name="mla_prep",
    )(x2, pos3, *consts)


def _attn_kernel(q_ref, k_ref, vt_ref, o_ref, m_ref, l_ref, acc_ref, sa_ref, sb_ref):
    tk = vt_ref.shape[2]
    nq, _, tq = q_ref.shape
    nh = q_ref.shape[1] // LANES
    key_idx = lax.broadcasted_iota(jnp.int32, (tk, 1), 0)
    qry_idx = lax.broadcasted_iota(jnp.int32, (1, tq), 1)

    def reset():
        m_ref[...] = jnp.full_like(m_ref, NEG_BIG)
        l_ref[...] = jnp.zeros_like(l_ref)
        acc_ref[...] = jnp.zeros_like(acc_ref)

    def scores(qi, t, s_ref):
        qi = jnp.minimum(qi, nq - 1)
        kblk = k_ref[pl.ds(pl.multiple_of(t * tk, tk), tk), :]
        qt = q_ref[qi]
        for h in range(nh):
            sl = slice(h * LANES, (h + 1) * LANES)
            s_ref[h] = _dot(kblk[:, sl], qt[sl, :])

    def consume(t, s_ref, visible):
        vt = vt_ref[t]
        for h in range(nh):
            s = s_ref[h] if visible is None else jnp.where(visible, s_ref[h], NEG_BIG)
            m_old = m_ref[h]
            m_new = jnp.maximum(m_old, jnp.max(s, axis=0, keepdims=True))
            p = jnp.exp2(s - m_new)
            alpha = jnp.exp2(m_old - m_new)
            l_ref[h] = alpha * l_ref[h] + jnp.sum(p, axis=0, keepdims=True)
            m_ref[h] = m_new
            acc_ref[h] = alpha * acc_ref[h] + _dot(vt[h * HEAD_DIM:(h + 1) * HEAD_DIM, :], p.astype(BF16))

    def finish(qi):
        out_t = jnp.concatenate([acc_ref[h] / l_ref[h] for h in range(nh)], axis=0)
        o_ref[pl.ds(pl.multiple_of(qi * tq, tq), tq), :] = out_t.T.astype(o_ref.dtype)
        reset()

    def succ(qi, t):
        last = t == qi
        return jnp.where(last, qi + 1, qi), jnp.where(last, 0, t + 1)

    def close(qi, t, s_ref, diagonal):
        consume(t, s_ref, key_idx <= qry_idx if diagonal else None)
        if diagonal:
            finish(qi)

    reset()
    scores(0, 0, sa_ref)

    def body(j, carry):
        qi, t = carry
        q1, t1 = succ(qi, t)
        q2, t2 = succ(q1, t1)
        for d0 in (False, True):
            for d1 in (False, True):
                @pl.when(((t == qi) == d0) & ((t1 == q1) == d1))
                def _(d0=d0, d1=d1):
                    scores(q1, t1, sb_ref)
                    close(qi, t, sa_ref, d0)
                    scores(q2, t2, sa_ref)
                    close(q1, t1, sb_ref, d1)
        return q2, t2

    total = nq * (nq + 1) // 2
    lax.fori_loop(0, total // 2, body, (jnp.int32(0), jnp.int32(0)))
    if total % 2:
        close(jnp.int32(nq - 1), jnp.int32(nq - 1), sa_ref, True)


def _mla_attn(qt, k, vt, batch, seq, heads, nh):
    T = k.shape[0]
    tk = vt.shape[2]
    nk = seq // tk
    qt4 = qt.reshape(batch, nk, heads * LANES, tk)
    vt4 = vt.reshape(batch, nk, heads * HEAD_DIM, tk)
    return pl.pallas_call(
        _attn_kernel,
        grid=(batch, heads // nh),
        in_specs=[pl.BlockSpec((None, nk, nh * LANES, tk), lambda b, p: (b, 0, p, 0)),
                  pl.BlockSpec((seq, nh * LANES), lambda b, p: (b, p)),
                  pl.BlockSpec((None, nk, nh * HEAD_DIM, tk), lambda b, p: (b, 0, p, 0))],
        out_specs=pl.BlockSpec((seq, nh * HEAD_DIM), lambda b, p: (b, p)),
        out_shape=jax.ShapeDtypeStruct((T, heads * HEAD_DIM), BF16),
        scratch_shapes=[pltpu.VMEM((nh, 1, tk), F32), pltpu.VMEM((nh, 1, tk), F32),
                        pltpu.VMEM((nh, HEAD_DIM, tk), F32),
                        pltpu.VMEM((nh, tk, tk), F32), pltpu.VMEM((nh, tk, tk), F32)],
        compiler_params=_params("arbitrary", "arbitrary"),
        name="mla_attn",
    )(qt4, k, vt4)


def _layer_norm(t, gain, bias):
    mu = jnp.mean(t, axis=-1, keepdims=True)
    c = t - mu
    var = jnp.mean(c * c, axis=-1, keepdims=True)
    return c * lax.rsqrt(var + LN_EPS) * gain + bias


def _mix_kernel(x_ref, y_ref, bonus_ref, g_ref, ym_ref, wg_ref, wpr_ref, wpm_ref, wo_ref, lnxg_ref, lnxb_ref,
                ln1g_ref, ln1b_ref, h_ref, *, alpha):
    tm, d = x_ref.shape
    parts = [slice(0, tm // 2), slice(tm // 2, tm)]
    gates = [_dot(x_ref[p, :].astype(BF16), wg_ref[...]) for p in parts]
    pm = [_dot(ym_ref[p, :], wpm_ref[...]) for p in parts]
    y_r = [((y_ref[p, :] * lnxg_ref[...] + lnxb_ref[...] + bonus_ref[p, :]) * g_ref[p, :]).astype(BF16) for p in parts]
    pr = [_dot(t, wpr_ref[...]) for t in y_r]
    merged = [(jax.nn.sigmoid(gt[:, 0:d]) * a + jax.nn.sigmoid(gt[:, d:2 * d]) * b).astype(BF16)
              for gt, a, b in zip(gates, pr, pm)]
    out = [_dot(t, wo_ref[...]) for t in merged]
    for p, o in zip(parts, out):
        h_ref[p, :] = _layer_norm(alpha * x_ref[p, :] + o, ln1g_ref[...], ln1b_ref[...])


def _mix(x2, y, bonus, g, ym, wg, wpr, wpm, wo, lnxg, lnxb, ln1g, ln1b, alpha, tm):
    T, D = x2.shape
    rdim = y.shape[1]
    row_spec = lambda w: pl.BlockSpec((tm, w), lambda i: (i, 0))
    consts = (wg, wpr, wpm, wo, lnxg, lnxb, ln1g, ln1b)
    return pl.pallas_call(
        functools.partial(_mix_kernel, alpha=alpha),
        grid=(T // tm,),
        in_specs=[row_spec(D), row_spec(rdim), row_spec(rdim), row_spec(rdim), row_spec(ym.shape[1])]
        + [_const_spec(c.shape) for c in consts],
        out_specs=row_spec(D),
        out_shape=jax.ShapeDtypeStruct((T, D), F32),
        compiler_params=_params("arbitrary"),
        name="mix",
    )(x2, y, bonus, g, ym, *consts)


def _ffn_kernel(h_ref, wg_ref, wu_ref, wd_ref, ln2g_ref, ln2b_ref, o_ref, *, alpha, ff_bounds):
    h = h_ref[...]
    hb = h.astype(BF16)
    acc = alpha * h
    chunks = list(zip(ff_bounds[:-1], ff_bounds[1:]))
    for lo, hi in chunks[:-1]:
        sl = slice(lo, hi)
        gate = _dot(hb, wg_ref[:, sl])
        act = (gate * jax.nn.sigmoid(gate) * _dot(hb, wu_ref[:, sl])).astype(BF16)
        acc = acc + _dot(act, wd_ref[sl, :])
    sl = slice(*chunks[-1])
    gate = _dot(hb, wg_ref[:, sl])
    act = (gate * jax.nn.sigmoid(gate) * _dot(hb, wu_ref[:, sl])).astype(BF16)
    tm = h.shape[0]
    th = min(tm, PART_ROWS)
    for r in range(0, tm, th):
        rows = slice(r, r + th)
        o_ref[rows, :] = _layer_norm(acc[rows, :] + _dot(act[rows, :], wd_ref[sl, :]), ln2g_ref[...], ln2b_ref[...])


def _ffn(h, wg, wu, wd, ln2g, ln2b, alpha, tm):
    T, D = h.shape
    dff = wg.shape[1]
    split = min(dff, -(-(dff // 2) // MXU_TILE) * MXU_TILE)
    ff_bounds = (0, split, dff) if split < dff else (0, dff)
    row_spec = pl.BlockSpec((tm, D), lambda i: (i, 0))
    consts = (wg, wu, wd, ln2g, ln2b)
    return pl.pallas_call(
        functools.partial(_ffn_kernel, alpha=alpha, ff_bounds=ff_bounds),
        grid=(T // tm,),
        in_specs=[row_spec] + [_const_spec(c.shape) for c in consts],
        out_specs=row_spec,
        out_shape=jax.ShapeDtypeStruct((T, D), F32),
        compiler_params=_params("arbitrary"),
        name="ffn",
    )(h, *consts)


def _layer(x2, pos2, batch, seq, alpha, w_in, mu_shift, w_decay_up, w_decay_base, w_aaa_up, w_aaa_base, w_gate_up,
           k_k, k_a, r_k, lnx_g, lnx_b, q_norm_g, w_uq, kv_norm_g, w_ukv, w_proj_rwkv, w_proj_mla, w_out,
           ln1_g, ln1_b, w_ffn_gate, w_ffn_up, w_ffn_down, ln2_g, ln2_b):
    D = x2.shape[1]
    rdim = k_k.shape[0]
    dl, al, gl = w_decay_up.shape[0], w_aaa_up.shape[0], w_gate_up.shape[0]
    q_lora, kv_lora = q_norm_g.shape[0], kv_norm_g.shape[0]
    heads = w_uq.shape[1] // (HEAD_DIM + ROPE_DIM)
    assert rdim % LANES == 0 and dl + al == LANES and gl == LANES and heads % 2 == 0
    assert w_ukv.shape[1] == heads * 2 * HEAD_DIM and w_proj_mla.shape[0] == heads * HEAD_DIM
    shift_cols = 3 * rdim + dl + al + gl
    tm = min(512, seq)
    assert seq % tm == 0 and tm % SCAN_CHUNK == 0
    row = lambda t: t.reshape(1, -1).astype(F32)

    ws = w_in[:, :shift_cols].astype(BF16)
    wdu = jnp.concatenate([w_decay_up, jnp.zeros((al, rdim), F32)], axis=0).astype(BF16)
    wau = jnp.concatenate([jnp.zeros((dl, rdim), F32), w_aaa_up], axis=0).astype(BF16)
    assert rdim % MXU_TILE == 0
    hid = jnp.arange(MXU_TILE) // HEAD_DIM
    hsum = (hid[:, None] == hid[None, :]).astype(BF16)
    rt, at, bt, kt, bh, kh, v, gc, bonus, g = _rwkv_prep(
        x2, ws, row(mu_shift), wdu, wau, w_gate_up.astype(BF16), row(w_decay_base), row(w_aaa_base), row(k_k),
        row(k_a), row(r_k), hsum, seq, tm)
    y = _rwkv_scan(rt, at, bt, kt, bh, kh, v, gc, batch, seq, 4 if batch % 4 == 0 else 1)

    o = shift_cols
    w_cq, w_ckv = w_in[:, o:o + q_lora], w_in[:, o + q_lora:o + q_lora + kv_lora]
    w_kpe = w_in[:, o + q_lora + kv_lora:o + q_lora + kv_lora + ROPE_DIM]
    pad_r = LANES - HEAD_DIM - ROPE_DIM
    wm = jnp.concatenate([w_cq, w_ckv, jnp.zeros((D, HEAD_DIM), F32), w_kpe, w_kpe[:, :HALF_ROPE],
                          jnp.zeros((D, pad_r - HALF_ROPE), F32)], axis=1).astype(BF16)
    wuq = jnp.pad(w_uq.reshape(q_lora, heads, HEAD_DIM + ROPE_DIM), ((0, 0), (0, 0), (0, pad_r)))
    wuq = wuq.reshape(q_lora, heads * LANES).T.astype(BF16)
    ukv = w_ukv.reshape(kv_lora, heads, 2 * HEAD_DIM)
    wuk = jnp.pad(ukv[:, :, :HEAD_DIM], ((0, 0), (0, 0), (0, LANES - HEAD_DIM))).reshape(kv_lora, heads * LANES)
    wuv = ukv[:, :, HEAD_DIM:].reshape(kv_lora, heads * HEAD_DIM).T
    inv_freq = ROPE_THETA ** (-jnp.arange(0, ROPE_DIM, 2, dtype=F32) / ROPE_DIM)
    lane = jnp.arange(LANES)[None, :]
    j = jnp.arange(HALF_ROPE)[:, None]
    on_x1, on_x2 = lane == HEAD_DIM + j, lane == HEAD_DIM + HALF_ROPE + j
    cplace = (on_x1 | on_x2).astype(BF16)
    splace = (on_x2.astype(F32) - on_x1.astype(F32)).astype(BF16)
    q, k, vv = _mla_prep(x2, pos2.reshape(-1, 1, tm), wm, row(q_norm_g), row(kv_norm_g), wuq, wuk.astype(BF16),
                         wuv.astype(BF16), inv_freq.reshape(HALF_ROPE, 1), cplace, splace, heads, tm)
    ym = _mla_attn(q, k, vv, batch, seq, heads, 4 if heads % 4 == 0 else 2)

    o2 = o + q_lora + kv_lora + ROPE_DIM
    wg = w_in[:, o2:o2 + 2 * D].astype(BF16)
    h = _mix(x2, y, bonus, g, ym, wg, w_proj_rwkv.astype(BF16), w_proj_mla.astype(BF16), w_out.astype(BF16),
             row(lnx_g), row(lnx_b), row(ln1_g), row(ln1_b), alpha, tm)
    return _ffn(h, w_ffn_gate.astype(BF16), w_ffn_up.astype(BF16), w_ffn_down.astype(BF16), row(ln2_g), row(ln2_b),
                alpha, tm)


def kernel(x, positions, w_in, mu_shift, w_decay_up, w_decay_base, w_aaa_up, w_aaa_base, w_gate_up, k_k, k_a, r_k,
           lnx_g, lnx_b, q_norm_g, w_uq, kv_norm_g, w_ukv, w_proj_rwkv, w_proj_mla, w_out, ln1_g, ln1_b,
           w_ffn_gate, w_ffn_up, w_ffn_down, ln2_g, ln2_b):
    batch, seq, D = x.shape
    depth = w_in.shape[0]
    alpha = (2.0 * depth) ** 0.25
    h = x.reshape(batch * seq, D)
    pos2 = positions.reshape(batch * seq, 1)
    per_layer = (w_in, mu_shift, w_decay_up, w_decay_base, w_aaa_up, w_aaa_base, w_gate_up, k_k, k_a, r_k, lnx_g,
                 lnx_b, q_norm_g, w_uq, kv_norm_g, w_ukv, w_proj_rwkv, w_proj_mla, w_out, ln1_g, ln1_b, w_ffn_gate,
                 w_ffn_up, w_ffn_down, ln2_g, ln2_b)
    for l in range(depth):
        layer_params = (p.reshape(p.shape[1:]) if depth == 1 else p[l] for p in per_layer)
        h = _layer(h, pos2, batch, seq, alpha, *layer_params)
    return h.reshape(batch, seq, D)
```

```python
import functools
import math

import jax
import jax.numpy as jnp
from jax import lax
from jax.experimental import pallas as pl
from jax.experimental.pallas import tpu as pltpu

F32 = jnp.float32
BF16 = jnp.bfloat16

HEAD_DIM = 64
ROPE_DIM = 32
HALF_ROPE = ROPE_DIM // 2
ROPE_THETA = 10000.0
GN_EPS = 64e-5
LN_EPS = 1e-5
RMS_EPS = 1e-6
NORM_EPS = 1e-12
LANES = 128
MXU_TILE = 256
PART_ROWS = 256
FF_CHUNK_ELEMS = 512 * 1408
SCAN_CHUNK = 64
VMEM_LIMIT = 56 * 1024 * 1024
NEG_BIG = -1e30


def _dot(a, b):
    return jnp.dot(a, b, preferred_element_type=F32)


def _dot_nt(a, b):
    return lax.dot_general(a, b, (((1,), (1,)), ((), ())), preferred_element_type=F32)


def _dot_tn(a, b):
    return lax.dot_general(a, b, (((0,), (0,)), ((), ())), preferred_element_type=F32)


def _split3(x):
    hi = x.astype(BF16)
    r1 = x - hi.astype(F32)
    mid = r1.astype(BF16)
    lo = (r1 - mid.astype(F32)).astype(BF16)
    return hi, mid, lo


def _head_sums(x, ones_blk):
    hi = x.astype(BF16)
    lo = (x - hi.astype(F32)).astype(BF16)
    w = ones_blk.shape[0]
    cols = [slice(c, c + w) for c in range(0, x.shape[1], w)]
    return jnp.concatenate([_dot(hi[:, c], ones_blk) + _dot(lo[:, c], ones_blk) for c in cols], axis=1)


def _sum_left(m, x):
    hi, mid, lo = _split3(x)
    return _dot(m, hi) + _dot(m, mid) + _dot(m, lo)


def _const_spec(shape):
    nd = len(shape)
    return pl.BlockSpec(shape, lambda *_: (0,) * nd, pipeline_mode=pl.Buffered(1))


def _params(*sem):
    return pltpu.CompilerParams(dimension_semantics=sem, vmem_limit_bytes=VMEM_LIMIT)


def _rwkv_prep_kernel(x_ref, ws_ref, mu_ref, wdu_ref, wau_ref, wgu_ref, dbase_ref, abase_ref, kk_ref, ka_ref,
                      rk_ref, hsum_ref, ctri_ref,
                      rt_ref, at_ref, bt_ref, kt_ref, bh_ref, kh_ref, v_ref, gc_ref, bonus_ref, g_ref,
                      carry_ref, *, tiles_per_seq, rdim):
    tm = x_ref.shape[0]
    first = (pl.program_id(0) % tiles_per_seq) == 0

    @pl.when(first)
    def _():
        carry_ref[...] = jnp.zeros_like(carry_ref)

    th = min(tm, PART_ROWS)
    parts = [slice(i, i + th) for i in range(0, tm, th)]
    xb = [x_ref[p, :].astype(BF16) for p in parts]
    row = lax.broadcasted_iota(jnp.int32, (th, 1), 0)

    def shifted(lo, hi):
        z = [_dot(t, ws_ref[:, lo:hi]) for t in xb]
        prev_last = [carry_ref[0:1, lo:hi]] + [t[th - 1:th, :] for t in z[:-1]]
        carry_ref[0:1, lo:hi] = z[-1][th - 1:th, :]
        out = []
        for zi, prev in zip(z, prev_last):
            z_prev = jnp.where(row == 0, prev, pltpu.roll(zi, 1, axis=0))
            out.append(zi + mu_ref[:, lo:hi] * (z_prev - zi))
        return out

    z_low = shifted(3 * rdim, 3 * rdim + 2 * LANES)
    z_lora = [t[:, 0:LANES] for t in z_low]
    z_gd = [t[:, LANES:2 * LANES] for t in z_low]
    z_k = shifted(rdim, 2 * rdim)
    d_up = [_dot(jnp.tanh(t).astype(BF16), wdu_ref[...]) for t in z_lora]
    a_up = [_dot(t.astype(BF16), wau_ref[...]) for t in z_lora]
    g_up = [_dot(jax.nn.sigmoid(t).astype(BF16), wgu_ref[...]) for t in z_gd]
    z_r = shifted(0, rdim)
    hsum = hsum_ref[...]
    kk_raw = [t * kk_ref[...] for t in z_k]
    kk_ss = [_head_sums(t * t, hsum) for t in kk_raw]
    z_v = shifted(2 * rdim, 3 * rdim)
    a = [jax.nn.sigmoid(abase_ref[...] + t) for t in a_up]
    kmod = [zk * (1.0 + (ai - 1.0) * ka_ref[...]) for zk, ai in zip(z_k, a)]
    bonus = [_head_sums(zr * km * rk_ref[...], hsum) for zr, km in zip(z_r, kmod)]

    ctri = ctri_ref[...]
    C = ctri.shape[0]
    for i, p in enumerate(parts):
        t = -(dbase_ref[...] + d_up[i])
        softplus = jnp.maximum(t, 0.0) + jnp.log(1.0 + jnp.exp(-jnp.abs(t)))
        logw = -jnp.exp(-softplus - 0.5)
        kk = kk_raw[i] * lax.rsqrt(jnp.maximum(kk_ss[i], NORM_EPS * NORM_EPS))
        b = kk * a[i]
        g_ref[p, :] = g_up[i]
        bonus_ref[p, :] = bonus[i] * z_v[i]
        v_ref[p, :] = z_v[i].astype(BF16)
        for c in range(th // C):
            sl = slice(c * C, (c + 1) * C)
            out = slice(p.start + c * C, p.start + (c + 1) * C)
            lw = logw[sl, :]
            cum = _sum_left(ctri, lw)
            e_neg = jnp.exp(-cum)
            chunk_decay = jnp.exp(cum[C - 1:C, :])
            gc_ref[out.start // C:out.start // C + 1, :] = chunk_decay
            b_neg = b[sl, :] * e_neg
            k_neg = kmod[i][sl, :] * e_neg
            rt_ref[out, :] = (z_r[i][sl, :] * jnp.exp(cum)).astype(BF16)
            at_ref[out, :] = (-kk[sl, :] * jnp.exp(cum - lw)).astype(BF16)
            bt_ref[out, :] = b_neg.astype(BF16)
            kt_ref[out, :] = k_neg.astype(BF16)
            bh_ref[out, :] = (b_neg * chunk_decay).astype(BF16)
            kh_ref[out, :] = (k_neg * chunk_decay).astype(BF16)


def _rwkv_prep(x2, ws, mu, wdu, wau, wgu, dbase, abase, k_k, k_a, r_k, hsum, seq, tm):
    T, D = x2.shape
    rdim = k_k.shape[-1]
    ncol = ws.shape[1]
    cpt = tm // SCAN_CHUNK
    r = jnp.arange(SCAN_CHUNK)
    ctri = (r[None, :] <= r[:, None]).astype(BF16)

    row_spec = lambda w: pl.BlockSpec((tm, w), lambda i: (i, 0))
    slab = jax.ShapeDtypeStruct((T, rdim), BF16)
    slab32 = jax.ShapeDtypeStruct((T, rdim), F32)
    out_shape = (slab,) * 7 + (jax.ShapeDtypeStruct((T // SCAN_CHUNK, rdim), F32), slab32, slab32)
    out_specs = (row_spec(rdim),) * 7 + (pl.BlockSpec((cpt, rdim), lambda i: (i, 0)), row_spec(rdim), row_spec(rdim))
    consts = (ws, mu, wdu, wau, wgu, dbase, abase, k_k, k_a, r_k, hsum, ctri)
    return pl.pallas_call(
        functools.partial(_rwkv_prep_kernel, tiles_per_seq=seq // tm, rdim=rdim),
        grid=(T // tm,),
        in_specs=[row_spec(D)] + [_const_spec(c.shape) for c in consts],
        out_specs=out_specs,
        out_shape=out_shape,
        scratch_shapes=[pltpu.VMEM((8, ncol), F32)],
        compiler_params=_params("arbitrary"),
        name="rwkv_prep",
    )(x2, *consts)


def _scan_kernel(rt_ref, at_ref, bt_ref, kt_ref, bh_ref, kh_ref, v_ref, gc_ref,
                 rt_nxt, at_nxt, bt_nxt, kt_nxt, v_nxt, y_ref, s_ref, inv_ref, akv_ref, ar_ref):
    C = SCAN_CHUNK
    nb = rt_ref.shape[0]
    npairs = rt_ref.shape[2] // LANES
    chains = [(g, p) for g in range(nb) for p in range(npairs)]
    c_idx = pl.program_id(1)
    last = pl.num_programs(1) - 1

    row = lax.broadcasted_iota(jnp.int32, (2 * C, 1), 0)
    lane = lax.broadcasted_iota(jnp.int32, (1, LANES), 1)
    keep = (row < C) == (lane < HEAD_DIM)
    t_idx = row % C
    s_idx = lane % C
    m_strict = keep & (s_idx < t_idx)
    m_incl = keep & (s_idx <= t_idx)
    eye = jnp.where(keep & (s_idx == t_idx), 1.0, 0.0)
    keep_b = jnp.where(keep, 1.0, 0.0).astype(BF16)

    def stacked(ref, g, p):
        x = ref[g, :, p * LANES:(p + 1) * LANES]
        return jnp.concatenate([x, x], axis=0) * keep_b

    def prepare(at, rt, bt, kt, v):
        gram = [_dot_nt(jnp.concatenate([stacked(at, g, p), stacked(rt, g, p)], axis=0),
                        jnp.concatenate([stacked(bt, g, p), stacked(kt, g, p)], axis=0)) for g, p in chains]
        a_ab = [jnp.where(m_strict, gm[0:2 * C, 0:LANES], 0.0) for gm in gram]
        a_ak = [jnp.where(m_strict, gm[0:2 * C, LANES:2 * LANES], 0.0).astype(BF16) for gm in gram]
        for i, gm in enumerate(gram):
            ar_ref[i] = jnp.concatenate([jnp.where(m_incl, gm[2 * C:4 * C, 0:LANES], 0.0),
                                         jnp.where(m_incl, gm[2 * C:4 * C, LANES:2 * LANES], 0.0)],
                                        axis=1).astype(BF16)
        inv = [eye + a for a in a_ab]
        pw = [a.astype(BF16) for a in a_ab]
        for _ in range(int(math.log2(C)) - 1):
            pw = [_dot(q, q).astype(BF16) for q in pw]
            inv = [x + _dot(x.astype(BF16), q) for x, q in zip(inv, pw)]
        akv = [_dot(ak, stacked(v, g, p)) for (g, p), ak in zip(chains, a_ak)]
        for i in range(len(chains)):
            inv_ref[i] = inv[i].astype(BF16)
            akv_ref[i] = akv[i]

    def advance():
        state = [s_ref[i] for i in range(len(chains))]
        state_b = [s.astype(BF16) for s in state]
        rhs = [_dot_nt(stacked(at_ref, g, p), sb) + akv_ref[i] for i, ((g, p), sb) in enumerate(zip(chains, state_b))]
        u_b = [_dot(inv_ref[i], r.astype(BF16)).astype(BF16) for i, r in enumerate(rhs)]
        uv = [jnp.concatenate([u, stacked(v_ref, g, p)], axis=0) for (g, p), u in zip(chains, u_b)]
        y = [_dot_nt(stacked(rt_ref, g, p), sb) + _dot(ar_ref[i], w)
             for i, ((g, p), sb, w) in enumerate(zip(chains, state_b, uv))]
        new_state = [_dot_tn(w, jnp.concatenate([stacked(bh_ref, g, p), stacked(kh_ref, g, p)], axis=0))
                     for (g, p), w in zip(chains, uv)]
        for i, (g, p) in enumerate(chains):
            sl = slice(p * LANES, (p + 1) * LANES)
            y_ref[g, :, sl] = y[i][0:C, :] + y[i][C:2 * C, :]
            s_ref[i] = state[i] * gc_ref[g, 0, :, sl] + new_state[i]

    @pl.when(c_idx == 0)
    def _():
        s_ref[...] = jnp.zeros_like(s_ref)
        prepare(at_ref, rt_ref, bt_ref, kt_ref, v_ref)

    @pl.when(c_idx < last)
    def _():
        advance()
        prepare(at_nxt, rt_nxt, bt_nxt, kt_nxt, v_nxt)

    @pl.when(c_idx == last)
    def _():
        advance()


def _rwkv_scan(rt, at, bt, kt, bh, kh, v, gc, batch, seq, nb):
    T, rdim = rt.shape
    C = SCAN_CHUNK
    nchunk = seq // C
    view = lambda t: t.reshape(batch, seq, rdim)
    slab_spec = pl.BlockSpec((nb, C, rdim), lambda b, c: (b, c, 0))
    next_spec = pl.BlockSpec((nb, C, rdim), lambda b, c: (b, jnp.minimum(c + 1, nchunk - 1), 0))
    nchains = nb * rdim // LANES
    y = pl.pallas_call(
        _scan_kernel,
        grid=(batch // nb, nchunk),
        in_specs=[slab_spec] * 7 + [pl.BlockSpec((nb, 1, 1, rdim), lambda b, c: (b, c, 0, 0))] + [next_spec] * 5,
        out_specs=slab_spec,
        out_shape=jax.ShapeDtypeStruct((batch, seq, rdim), F32),
        scratch_shapes=[pltpu.VMEM((nchains, LANES, LANES), F32), pltpu.VMEM((nchains, LANES, LANES), BF16),
                        pltpu.VMEM((nchains, LANES, LANES), F32), pltpu.VMEM((nchains, LANES, 2 * LANES), BF16)],
        compiler_params=_params("arbitrary", "arbitrary"),
        name="rwkv_scan",
    )(*(view(t) for t in (rt, at, bt, kt, bh, kh, v)), gc.reshape(batch, nchunk, 1, rdim),
      *(view(t) for t in (rt, at, bt, kt, v)))
    return y.reshape(T, rdim)


def _mla_prep_kernel(x_ref, pos_ref, wm_ref, qg_ref, kvg_ref, wuq_ref, wuk_ref, wuv_ref, freq_ref, cplace_ref,
                     splace_ref, q_ref, k_ref, v_ref, *, q_lora, kv_lora, heads):
    tm = x_ref.shape[0]
    parts = [slice(i * (tm // 2), (i + 1) * (tm // 2)) for i in range(2)]

    def rms(t, gain):
        return t * lax.rsqrt(jnp.mean(t * t, axis=-1, keepdims=True) + RMS_EPS) * gain

    z = [_dot(x_ref[p, :].astype(BF16), wm_ref[...]) for p in parts]
    qn = [rms(t[:, 0:q_lora], qg_ref[...]).astype(BF16) for t in z]
    kvn = [rms(t[:, q_lora:q_lora + kv_lora], kvg_ref[...]).astype(BF16) for t in z]
    qt = [_dot_nt(wuq_ref[...], t) for t in qn]
    kn = [_dot(t, wuk_ref[...]) for t in kvn]
    vt = [_dot_nt(wuv_ref[...], t) for t in kvn]

    q_scale = (HEAD_DIM + ROPE_DIM) ** -0.5 * math.log2(math.e)
    x1, x2 = slice(HEAD_DIM, HEAD_DIM + HALF_ROPE), slice(HEAD_DIM + HALF_ROPE, HEAD_DIM + ROPE_DIM)
    for i, p in enumerate(parts):
        v_ref[0, :, p] = vt[i].astype(BF16)
        ang_t = freq_ref[...] * pos_ref[0, :, p].astype(F32)
        cos_t, sin_t = jnp.cos(ang_t), jnp.sin(ang_t)
        for h in range(heads):
            blk = qt[i][h * LANES:(h + 1) * LANES, :]
            rot = jnp.concatenate([blk[0:HEAD_DIM, :], blk[x1, :] * cos_t - blk[x2, :] * sin_t,
                                   blk[x2, :] * cos_t + blk[x1, :] * sin_t, blk[HEAD_DIM + ROPE_DIM:, :]], axis=0)
            q_ref[0, h * LANES:(h + 1) * LANES, p] = (rot * q_scale).astype(BF16)
        cs = sum(_dot_tn(t, cplace_ref[...]) for t in _split3(cos_t))
        sn = sum(_dot_tn(t, splace_ref[...]) for t in _split3(sin_t))
        kpe = z[i][:, q_lora + kv_lora:q_lora + kv_lora + LANES]
        kpe_rot = kpe * cs + pltpu.roll(kpe, LANES - HALF_ROPE, axis=1) * sn
        for h in range(heads):
            sl = slice(h * LANES, (h + 1) * LANES)
            k_ref[p, sl] = (kn[i][:, sl] + kpe_rot).astype(BF16)


def _mla_prep(x2, pos3, wm, qg, kvg, wuq, wuk, wuv, freq, cplace, splace, heads, tm):
    T, D = x2.shape
    q_lora, kv_lora = qg.shape[-1], kvg.shape[-1]
    row_spec = lambda w: pl.BlockSpec((tm, w), lambda i: (i, 0))
    tile_t = lambda w: pl.BlockSpec((1, w, tm), lambda i: (i, 0, 0))
    consts = (wm, qg, kvg, wuq, wuk, wuv, freq, cplace, splace)
    return pl.pallas_call(
        functools.partial(_mla_prep_kernel, q_lora=q_lora, kv_lora=kv_lora, heads=heads),
        grid=(T // tm,),
        in_specs=[row_spec(D), pl.BlockSpec((1, 1, tm), lambda i: (i, 0, 0))] + [_const_spec(c.shape) for c in consts],
        out_specs=(tile_t(heads * LANES), row_spec(heads * LANES), tile_t(heads * HEAD_DIM)),
        out_shape=(jax.ShapeDtypeStruct((T // tm, heads * LANES, tm), BF16),
                   jax.ShapeDtypeStruct((T, heads * LANES), BF16),
                   jax.ShapeDtypeStruct((T // tm, heads * HEAD_DIM, tm), BF16)),
        compiler_params=_params("arbitrary"),
        name="mla_prep",
    )(x2, pos3, *consts)


def _attn_kernel(q_ref, k_ref, vt_ref, o_ref, m_ref, l_ref, acc_ref, sa_ref, sb_ref):
    tk = vt_ref.shape[2]
    nq, _, tq = q_ref.shape
    nh = q_ref.shape[1] // LANES
    key_idx = lax.broadcasted_iota(jnp.int32, (tk, 1), 0)
    qry_idx = lax.broadcasted_iota(jnp.int32, (1, tq), 1)

    def reset():
        m_ref[...] = jnp.full_like(m_ref, NEG_BIG)
        l_ref[...] = jnp.zeros_like(l_ref)
        acc_ref[...] = jnp.zeros_like(acc_ref)

    def scores(qi, t, s_ref):
        qi = jnp.minimum(qi, nq - 1)
        kblk = k_ref[pl.ds(pl.multiple_of(t * tk, tk), tk), :]
        qt = q_ref[qi]
        for h in range(nh):
            sl = slice(h * LANES, (h + 1) * LANES)
            s_ref[h] = _dot(kblk[:, sl], qt[sl, :])

    def consume(t, s_ref, visible):
        vt = vt_ref[t]
        for h in range(nh):
            s = s_ref[h] if visible is None else jnp.where(visible, s_ref[h], NEG_BIG)
            m_old = m_ref[h]
            m_new = jnp.maximum(m_old, jnp.max(s, axis=0, keepdims=True))
            p = jnp.exp2(s - m_new)
            alpha = jnp.exp2(m_old - m_new)
            l_ref[h] = alpha * l_ref[h] + jnp.sum(p, axis=0, keepdims=True)
            m_ref[h] = m_new
            acc_ref[h] = alpha * acc_ref[h] + _dot(vt[h * HEAD_DIM:(h + 1) * HEAD_DIM, :], p.astype(BF16))

    def finish(qi):
        out_t = jnp.concatenate([acc_ref[h] / l_ref[h] for h in range(nh)], axis=0)
        o_ref[pl.ds(pl.multiple_of(qi * tq, tq), tq), :] = out_t.T.astype(o_ref.dtype)
        reset()

    def succ(qi, t):
        last = t == qi
        return jnp.where(last, qi + 1, qi), jnp.where(last, 0, t + 1)

    def close(qi, t, s_ref, diagonal):
        consume(t, s_ref, key_idx <= qry_idx if diagonal else None)
        if diagonal:
            finish(qi)

    reset()
    scores(0, 0, sa_ref)

    def body(j, carry):
        qi, t = carry
        q1, t1 = succ(qi, t)
        q2, t2 = succ(q1, t1)
        for d0 in (False, True):
            for d1 in (False, True):
                @pl.when(((t == qi) == d0) & ((t1 == q1) == d1))
                def _(d0=d0, d1=d1):
                    scores(q1, t1, sb_ref)
                    close(qi, t, sa_ref, d0)
                    scores(q2, t2, sa_ref)
                    close(q1, t1, sb_ref, d1)
        return q2, t2

    total = nq * (nq + 1) // 2
    lax.fori_loop(0, total // 2, body, (jnp.int32(0), jnp.int32(0)))
    if total % 2:
        close(jnp.int32(nq - 1), jnp.int32(nq - 1), sa_ref, True)


def _mla_attn(qt, k, vt, batch, seq, heads, nh):
    T = k.shape[0]
    tk = vt.shape[2]
    nk = seq // tk
    qt4 = qt.reshape(batch, nk, heads * LANES, tk)
    vt4 = vt.reshape(batch, nk, heads * HEAD_DIM, tk)
    return pl.pallas_call(
        _attn_kernel,
        grid=(batch, heads // nh),
        in_specs=[pl.BlockSpec((None, nk, nh * LANES, tk), lambda b, p: (b, 0, p, 0)),
                  pl.BlockSpec((seq, nh * LANES), lambda b, p: (b, p)),
                  pl.BlockSpec((None, nk, nh * HEAD_DIM, tk), lambda b, p: (b, 0, p, 0))],
        out_specs=pl.BlockSpec((seq, nh * HEAD_DIM), lambda b, p: (b, p)),
        out_shape=jax.ShapeDtypeStruct((T, heads * HEAD_DIM), BF16),
        scratch_shapes=[pltpu.VMEM((nh, 1, tk), F32), pltpu.VMEM((nh, 1, tk), F32),
                        pltpu.VMEM((nh, HEAD_DIM, tk), F32),
                        pltpu.VMEM((nh, tk, tk), F32), pltpu.VMEM((nh, tk, tk), F32)],
        compiler_params=_params("arbitrary", "arbitrary"),
        name="mla_attn",
    )(qt4, k, vt4)


def _layer_norm(t, gain, bias):
    mu = jnp.mean(t, axis=-1, keepdims=True)
    c = t - mu
    var = jnp.mean(c * c, axis=-1, keepdims=True)
    return c * lax.rsqrt(var + LN_EPS) * gain + bias


def _mix_kernel(x_ref, y_ref, bonus_ref, g_ref, ym_ref, wg_ref, wpr_ref, wpm_ref, wo_ref, lnxg_ref, lnxb_ref,
                ln1g_ref, ln1b_ref, h_ref, *, alpha):
    tm, d = x_ref.shape
    parts = [slice(0, tm // 2), slice(tm // 2, tm)]
    gates = [_dot(x_ref[p, :].astype(BF16), wg_ref[...]) for p in parts]
    pm = [_dot(ym_ref[p, :], wpm_ref[...]) for p in parts]
    lane = lax.broadcasted_iota(jnp.int32, (1, LANES), 1)
    low = lane < HEAD_DIM
    inv_n = 1.0 / HEAD_DIM

    def head_sums(t):
        s_all = jnp.sum(t, axis=1, keepdims=True)
        s_low = jnp.sum(jnp.where(low, t, 0.0), axis=1, keepdims=True)
        return jnp.where(low, s_low, s_all - s_low)

    def group_norm(t):
        cols = []
        for c in range(0, t.shape[1], LANES):
            blk = t[:, c:c + LANES]
            cen = blk - head_sums(blk) * inv_n
            cols.append(cen * lax.rsqrt(head_sums(cen * cen) * inv_n + GN_EPS))
        return jnp.concatenate(cols, axis=1)

    y_r = [((group_norm(y_ref[p, :]) * lnxg_ref[...] + lnxb_ref[...] + bonus_ref[p, :]) * g_ref[p, :]).astype(BF16)
           for p in parts]
    pr = [_dot(t, wpr_ref[...]) for t in y_r]
    merged = [(jax.nn.sigmoid(gt[:, 0:d]) * a + jax.nn.sigmoid(gt[:, d:2 * d]) * b).astype(BF16)
              for gt, a, b in zip(gates, pr, pm)]
    out = [_dot(t, wo_ref[...]) for t in merged]
    for p, o in zip(parts, out):
        h_ref[p, :] = _layer_norm(alpha * x_ref[p, :] + o, ln1g_ref[...], ln1b_ref[...])


def _mix(x2, y, bonus, g, ym, wg, wpr, wpm, wo, lnxg, lnxb, ln1g, ln1b, alpha, tm):
    T, D = x2.shape
    rdim = y.shape[1]
    row_spec = lambda w: pl.BlockSpec((tm, w), lambda i: (i, 0))
    consts = (wg, wpr, wpm, wo, lnxg, lnxb, ln1g, ln1b)
    return pl.pallas_call(
        functools.partial(_mix_kernel, alpha=alpha),
        grid=(T // tm,),
        in_specs=[row_spec(D), row_spec(rdim), row_spec(rdim), row_spec(rdim), row_spec(ym.shape[1])]
        + [_const_spec(c.shape) for c in consts],
        out_specs=row_spec(D),
        out_shape=jax.ShapeDtypeStruct((T, D), F32),
        compiler_params=_params("arbitrary"),
        name="mix",
    )(x2, y, bonus, g, ym, *consts)


def _ffn_kernel(h_ref, wg_ref, wu_ref, wd_ref, ln2g_ref, ln2b_ref, o_ref, *, alpha, ff_bounds):
    h = h_ref[...]
    hb = h.astype(BF16)
    acc = alpha * h
    chunks = list(zip(ff_bounds[:-1], ff_bounds[1:]))
    for lo, hi in chunks[:-1]:
        sl = slice(lo, hi)
        gate = _dot(hb, wg_ref[:, sl])
        act = (gate * jax.nn.sigmoid(gate) * _dot(hb, wu_ref[:, sl])).astype(BF16)
        acc = acc + _dot(act, wd_ref[sl, :])
    sl = slice(*chunks[-1])
    gate = _dot(hb, wg_ref[:, sl])
    act = (gate * jax.nn.sigmoid(gate) * _dot(hb, wu_ref[:, sl])).astype(BF16)
    tm = h.shape[0]
    th = min(tm, PART_ROWS)
    for r in range(0, tm, th):
        rows = slice(r, r + th)
        o_ref[rows, :] = _layer_norm(acc[rows, :] + _dot(act[rows, :], wd_ref[sl, :]), ln2g_ref[...], ln2b_ref[...])


def _ffn(h, wg, wu, wd, ln2g, ln2b, alpha, tm):
    T, D = h.shape
    dff = wg.shape[1]
    nchunks = max(1, round(dff * tm / FF_CHUNK_ELEMS))
    width = -(-dff // (nchunks * MXU_TILE)) * MXU_TILE
    ff_bounds = tuple(range(0, dff, width)) + (dff,)
    row_spec = pl.BlockSpec((tm, D), lambda i: (i, 0))
    consts = (wg, wu, wd, ln2g, ln2b)
    return pl.pallas_call(
        functools.partial(_ffn_kernel, alpha=alpha, ff_bounds=ff_bounds),
        grid=(T // tm,),
        in_specs=[row_spec] + [_const_spec(c.shape) for c in consts],
        out_specs=row_spec,
        out_shape=jax.ShapeDtypeStruct((T, D), F32),
        compiler_params=_params("arbitrary"),
        name="ffn",
    )(h, *consts)


def _layer(x2, pos2, batch, seq, alpha, w_in, mu_shift, w_decay_up, w_decay_base, w_aaa_up, w_aaa_base, w_gate_up,
           k_k, k_a, r_k, lnx_g, lnx_b, q_norm_g, w_uq, kv_norm_g, w_ukv, w_proj_rwkv, w_proj_mla, w_out,
           ln1_g, ln1_b, w_ffn_gate, w_ffn_up, w_ffn_down, ln2_g, ln2_b):
    D = x2.shape[1]
    rdim = k_k.shape[0]
    dl, al, gl = w_decay_up.shape[0], w_aaa_up.shape[0], w_gate_up.shape[0]
    q_lora, kv_lora = q_norm_g.shape[0], kv_norm_g.shape[0]
    heads = w_uq.shape[1] // (HEAD_DIM + ROPE_DIM)
    assert rdim % LANES == 0 and dl + al == LANES and gl == LANES and heads % 2 == 0
    assert w_ukv.shape[1] == heads * 2 * HEAD_DIM and w_proj_mla.shape[0] == heads * HEAD_DIM
    shift_cols = 3 * rdim + dl + al + gl
    tm = min(512, seq)
    assert seq % tm == 0 and tm % SCAN_CHUNK == 0
    row = lambda t: t.reshape(1, -1).astype(F32)

    ws = w_in[:, :shift_cols].astype(BF16)
    wdu = jnp.concatenate([w_decay_up, jnp.zeros((al, rdim), F32)], axis=0).astype(BF16)
    wau = jnp.concatenate([jnp.zeros((dl, rdim), F32), w_aaa_up], axis=0).astype(BF16)
    assert rdim % MXU_TILE == 0
    hid = jnp.arange(MXU_TILE) // HEAD_DIM
    hsum = (hid[:, None] == hid[None, :]).astype(BF16)
    rt, at, bt, kt, bh, kh, v, gc, bonus, g = _rwkv_prep(
        x2, ws, row(mu_shift), wdu, wau, w_gate_up.astype(BF16), row(w_decay_base), row(w_aaa_base), row(k_k),
        row(k_a), row(r_k), hsum, seq, tm)
    y = _rwkv_scan(rt, at, bt, kt, bh, kh, v, gc, batch, seq, 4 if batch % 4 == 0 else 1)

    o = shift_cols
    w_cq, w_ckv = w_in[:, o:o + q_lora], w_in[:, o + q_lora:o + q_lora + kv_lora]
    w_kpe = w_in[:, o + q_lora + kv_lora:o + q_lora + kv_lora + ROPE_DIM]
    pad_r = LANES - HEAD_DIM - ROPE_DIM
    wm = jnp.concatenate([w_cq, w_ckv, jnp.zeros((D, HEAD_DIM), F32), w_kpe, w_kpe[:, :HALF_ROPE],
                          jnp.zeros((D, pad_r - HALF_ROPE), F32)], axis=1).astype(BF16)
    wuq = jnp.pad(w_uq.reshape(q_lora, heads, HEAD_DIM + ROPE_DIM), ((0, 0), (0, 0), (0, pad_r)))
    wuq = wuq.reshape(q_lora, heads * LANES).T.astype(BF16)
    ukv = w_ukv.reshape(kv_lora, heads, 2 * HEAD_DIM)
    wuk = jnp.pad(ukv[:, :, :HEAD_DIM], ((0, 0), (0, 0), (0, LANES - HEAD_DIM))).reshape(kv_lora, heads * LANES)
    wuv = ukv[:, :, HEAD_DIM:].reshape(kv_lora, heads * HEAD_DIM).T
    inv_freq = ROPE_THETA ** (-jnp.arange(0, ROPE_DIM, 2, dtype=F32) / ROPE_DIM)
    lane = jnp.arange(LANES)[None, :]
    j = jnp.arange(HALF_ROPE)[:, None]
    on_x1, on_x2 = lane == HEAD_DIM + j, lane == HEAD_DIM + HALF_ROPE + j
    cplace = (on_x1 | on_x2).astype(BF16)
    splace = (on_x2.astype(F32) - on_x1.astype(F32)).astype(BF16)
    q, k, vv = _mla_prep(x2, pos2.reshape(-1, 1, tm), wm, row(q_norm_g), row(kv_norm_g), wuq, wuk.astype(BF16),
                         wuv.astype(BF16), inv_freq.reshape(HALF_ROPE, 1), cplace, splace, heads, tm)
    ym = _mla_attn(q, k, vv, batch, seq, heads, 4 if heads % 4 == 0 else 2)

    o2 = o + q_lora + kv_lora + ROPE_DIM
    wg = w_in[:, o2:o2 + 2 * D].astype(BF16)
    h = _mix(x2, y, bonus, g, ym, wg, w_proj_rwkv.astype(BF16), w_proj_mla.astype(BF16), w_out.astype(BF16),
             row(lnx_g), row(lnx_b), row(ln1_g), row(ln1_b), alpha, tm)
    return _ffn(h, w_ffn_gate.astype(BF16), w_ffn_up.astype(BF16), w_ffn_down.astype(BF16), row(ln2_g), row(ln2_b),
                alpha, tm)


def kernel(x, positions, w_in, mu_shift, w_decay_up, w_decay_base, w_aaa_up, w_aaa_base, w_gate_up, k_k, k_a, r_k,
           lnx_g, lnx_b, q_norm_g, w_uq, kv_norm_g, w_ukv, w_proj_rwkv, w_proj_mla, w_out, ln1_g, ln1_b,
           w_ffn_gate, w_ffn_up, w_ffn_down, ln2_g, ln2_b):
    batch, seq, D = x.shape
    depth = w_in.shape[0]
    alpha = (2.0 * depth) ** 0.25
    h = x.reshape(batch * seq, D)
    pos2 = positions.reshape(batch * seq, 1)
    per_layer = (w_in, mu_shift, w_decay_up, w_decay_base, w_aaa_up, w_aaa_base, w_gate_up, k_k, k_a, r_k, lnx_g,
                 lnx_b, q_norm_g, w_uq, kv_norm_g, w_ukv, w_proj_rwkv, w_proj_mla, w_out, ln1_g, ln1_b, w_ffn_gate,
                 w_ffn_up, w_ffn_down, ln2_g, ln2_b)
    for l in range(depth):
        layer_params = (p.reshape(p.shape[1:]) if depth == 1 else p[l] for p in per_layer)
        h = _layer(h, pos2, batch, seq, alpha, *layer_params)
    return h.reshape(batch, seq, D)
```

```python
import functools
import math

import jax
import jax.numpy as jnp
from jax import lax
from jax.experimental import pallas as pl
from jax.experimental.pallas import tpu as pltpu

F32 = jnp.float32
BF16 = jnp.bfloat16

HEAD_DIM = 64
ROPE_DIM = 32
HALF_ROPE = ROPE_DIM // 2
ROPE_THETA = 10000.0
GN_EPS = 64e-5
LN_EPS = 1e-5
RMS_EPS = 1e-6
NORM_EPS = 1e-12
LANES = 128
MXU_TILE = 256
PART_ROWS = 256
FF_CHUNK_ELEMS = 512 * 1408
SCAN_CHUNK = 64
VMEM_LIMIT = 56 * 1024 * 1024
NEG_BIG = -1e30


def _dot(a, b):
    return jnp.dot(a, b, preferred_element_type=F32)


def _dot_nt(a, b):
    return lax.dot_general(a, b, (((1,), (1,)), ((), ())), preferred_element_type=F32)


def _dot_tn(a, b):
    return lax.dot_general(a, b, (((0,), (0,)), ((), ())), preferred_element_type=F32)


def _split3(x):
    hi = x.astype(BF16)
    r1 = x - hi.astype(F32)
    mid = r1.astype(BF16)
    lo = (r1 - mid.astype(F32)).astype(BF16)
    return hi, mid, lo


def _head_sums(x, ones_blk):
    hi = x.astype(BF16)
    lo = (x - hi.astype(F32)).astype(BF16)
    w = ones_blk.shape[0]
    cols = [slice(c, c + w) for c in range(0, x.shape[1], w)]
    return jnp.concatenate([_dot(hi[:, c], ones_blk) + _dot(lo[:, c], ones_blk) for c in cols], axis=1)


def _sum_left(m, x):
    hi, mid, lo = _split3(x)
    return _dot(m, hi) + _dot(m, mid) + _dot(m, lo)


def _const_spec(shape):
    nd = len(shape)
    return pl.BlockSpec(shape, lambda *_: (0,) * nd, pipeline_mode=pl.Buffered(1))


def _params(*sem):
    return pltpu.CompilerParams(dimension_semantics=sem, vmem_limit_bytes=VMEM_LIMIT)


def _rwkv_prep_kernel(x_ref, ws_ref, mu_ref, wdu_ref, wau_ref, wgu_ref, dbase_ref, abase_ref, kk_ref, ka_ref,
                      rk_ref, hsum_ref, ctri_ref,
                      rt_ref, at_ref, bt_ref, kt_ref, bh_ref, kh_ref, v_ref, gc_ref, bonus_ref, g_ref,
                      carry_ref, *, tiles_per_seq, rdim):
    tm = x_ref.shape[0]
    first = (pl.program_id(0) % tiles_per_seq) == 0

    @pl.when(first)
    def _():
        carry_ref[...] = jnp.zeros_like(carry_ref)

    th = min(tm, PART_ROWS)
    parts = [slice(i, i + th) for i in range(0, tm, th)]
    xb = [x_ref[p, :].astype(BF16) for p in parts]
    row = lax.broadcasted_iota(jnp.int32, (th, 1), 0)

    def shifted(lo, hi):
        z = [_dot(t, ws_ref[:, lo:hi]) for t in xb]
        prev_last = [carry_ref[0:1, lo:hi]] + [t[th - 1:th, :] for t in z[:-1]]
        carry_ref[0:1, lo:hi] = z[-1][th - 1:th, :]
        out = []
        for zi, prev in zip(z, prev_last):
            z_prev = jnp.where(row == 0, prev, pltpu.roll(zi, 1, axis=0))
            out.append(zi + mu_ref[:, lo:hi] * (z_prev - zi))
        return out

    z_low = shifted(3 * rdim, 3 * rdim + 2 * LANES)
    z_lora = [t[:, 0:LANES] for t in z_low]
    z_gd = [t[:, LANES:2 * LANES] for t in z_low]
    z_k = shifted(rdim, 2 * rdim)
    d_up = [_dot(jnp.tanh(t).astype(BF16), wdu_ref[...]) for t in z_lora]
    a_up = [_dot(t.astype(BF16), wau_ref[...]) for t in z_lora]
    g_up = [_dot(jax.nn.sigmoid(t).astype(BF16), wgu_ref[...]) for t in z_gd]
    z_r = shifted(0, rdim)
    hsum = hsum_ref[...]
    kk_raw = [t * kk_ref[...] for t in z_k]
    kk_ss = [_head_sums(t * t, hsum) for t in kk_raw]
    z_v = shifted(2 * rdim, 3 * rdim)
    a = [jax.nn.sigmoid(abase_ref[...] + t) for t in a_up]
    kmod = [zk * (1.0 + (ai - 1.0) * ka_ref[...]) for zk, ai in zip(z_k, a)]
    bonus = [_head_sums(zr * km * rk_ref[...], hsum) for zr, km in zip(z_r, kmod)]

    ctri = ctri_ref[...]
    C = ctri.shape[0]
    for i, p in enumerate(parts):
        t = -(dbase_ref[...] + d_up[i])
        softplus = jnp.maximum(t, 0.0) + jnp.log(1.0 + jnp.exp(-jnp.abs(t)))
        logw = -jnp.exp(-softplus - 0.5)
        kk = kk_raw[i] * lax.rsqrt(jnp.maximum(kk_ss[i], NORM_EPS * NORM_EPS))
        b = kk * a[i]
        g_ref[p, :] = g_up[i]
        bonus_ref[p, :] = bonus[i] * z_v[i]
        v_ref[p, :] = z_v[i].astype(BF16)
        for c in range(th // C):
            sl = slice(c * C, (c + 1) * C)
            out = slice(p.start + c * C, p.start + (c + 1) * C)
            lw = logw[sl, :]
            cum = _sum_left(ctri, lw)
            e_neg = jnp.exp(-cum)
            chunk_decay = jnp.exp(cum[C - 1:C, :])
            gc_ref[out.start // C:out.start // C + 1, :] = chunk_decay
            b_neg = b[sl, :] * e_neg
            k_neg = kmod[i][sl, :] * e_neg
            rt_ref[out, :] = (z_r[i][sl, :] * jnp.exp(cum)).astype(BF16)
            at_ref[out, :] = (-kk[sl, :] * jnp.exp(cum - lw)).astype(BF16)
            bt_ref[out, :] = b_neg.astype(BF16)
            kt_ref[out, :] = k_neg.astype(BF16)
            bh_ref[out, :] = (b_neg * chunk_decay).astype(BF16)
            kh_ref[out, :] = (k_neg * chunk_decay).astype(BF16)


def _rwkv_prep(x2, ws, mu, wdu, wau, wgu, dbase, abase, k_k, k_a, r_k, hsum, seq, tm):
    T, D = x2.shape
    rdim = k_k.shape[-1]
    ncol = ws.shape[1]
    cpt = tm // SCAN_CHUNK
    r = jnp.arange(SCAN_CHUNK)
    ctri = (r[None, :] <= r[:, None]).astype(BF16)

    row_spec = lambda w: pl.BlockSpec((tm, w), lambda i: (i, 0))
    slab = jax.ShapeDtypeStruct((T, rdim), BF16)
    slab32 = jax.ShapeDtypeStruct((T, rdim), F32)
    out_shape = (slab,) * 7 + (jax.ShapeDtypeStruct((T // SCAN_CHUNK, rdim), F32), slab32, slab32)
    out_specs = (row_spec(rdim),) * 7 + (pl.BlockSpec((cpt, rdim), lambda i: (i, 0)), row_spec(rdim), row_spec(rdim))
    consts = (ws, mu, wdu, wau, wgu, dbase, abase, k_k, k_a, r_k, hsum, ctri)
    return pl.pallas_call(
        functools.partial(_rwkv_prep_kernel, tiles_per_seq=seq // tm, rdim=rdim),
        grid=(T // tm,),
        in_specs=[row_spec(D)] + [_const_spec(c.shape) for c in consts],
        out_specs=out_specs,
        out_shape=out_shape,
        scratch_shapes=[pltpu.VMEM((8, ncol), F32)],
        compiler_params=_params("arbitrary"),
        name="rwkv_prep",
    )(x2, *consts)


def _scan_kernel(rt_ref, at_ref, bt_ref, kt_ref, bh_ref, kh_ref, v_ref, gc_ref,
                 rt_nxt, at_nxt, bt_nxt, kt_nxt, bh_nxt, kh_nxt, v_nxt, gc_nxt,
                 y_ref, s_ref, inv_ref, akv_ref, ar_ref, bkt_ref, gcol_ref):
    C = SCAN_CHUNK
    nb = rt_ref.shape[0]
    npairs = rt_ref.shape[2] // LANES
    chains = [(g, p) for g in range(nb) for p in range(npairs)]
    c_idx = pl.program_id(1)
    last = pl.num_programs(1) - 1

    row = lax.broadcasted_iota(jnp.int32, (2 * C, 1), 0)
    lane = lax.broadcasted_iota(jnp.int32, (1, LANES), 1)
    keep = (row < C) == (lane < HEAD_DIM)
    t_idx = row % C
    s_idx = lane % C
    m_strict = keep & (s_idx < t_idx)
    m_incl = keep & (s_idx <= t_idx)
    eye = jnp.where(keep & (s_idx == t_idx), 1.0, 0.0)
    keep_b = jnp.where(keep, 1.0, 0.0).astype(BF16)

    def stacked(ref, g, p):
        x = ref[g, :, p * LANES:(p + 1) * LANES]
        return jnp.concatenate([x, x], axis=0) * keep_b

    def prepare(at, rt, bt, kt, bh, kh, v, gc):
        for i, (g, p) in enumerate(chains):
            bk = jnp.concatenate([stacked(bh, g, p), stacked(kh, g, p)], axis=0)
            bkt_ref[i] = bk.astype(F32).T.astype(BF16)
            decay = gc[g, 0, :, p * LANES:(p + 1) * LANES]
            gcol_ref[i] = jnp.broadcast_to(decay, (LANES, LANES)).T
        gram = [_dot_nt(jnp.concatenate([stacked(at, g, p), stacked(rt, g, p)], axis=0),
                        jnp.concatenate([stacked(bt, g, p), stacked(kt, g, p)], axis=0)) for g, p in chains]
        a_ab = [jnp.where(m_strict, gm[0:2 * C, 0:LANES], 0.0) for gm in gram]
        a_ak = [jnp.where(m_strict, gm[0:2 * C, LANES:2 * LANES], 0.0).astype(BF16) for gm in gram]
        for i, gm in enumerate(gram):
            ar_ref[i] = jnp.concatenate([jnp.where(m_incl, gm[2 * C:4 * C, 0:LANES], 0.0),
                                         jnp.where(m_incl, gm[2 * C:4 * C, LANES:2 * LANES], 0.0)],
                                        axis=1).astype(BF16)
        inv = [eye + a for a in a_ab]
        pw = [a.astype(BF16) for a in a_ab]
        for _ in range(int(math.log2(C)) - 1):
            pw = [_dot(q, q).astype(BF16) for q in pw]
            inv = [x + _dot(x.astype(BF16), q) for x, q in zip(inv, pw)]
        akv = [_dot(ak, stacked(v, g, p)) for (g, p), ak in zip(chains, a_ak)]
        for i in range(len(chains)):
            inv_ref[i] = inv[i].astype(BF16)
            akv_ref[i] = akv[i]

    def advance():
        state = [s_ref[i] for i in range(len(chains))]
        state_b = [s.astype(BF16) for s in state]
        rhs = [_dot(stacked(at_ref, g, p), sb) + akv_ref[i] for i, ((g, p), sb) in enumerate(zip(chains, state_b))]
        u_b = [_dot(inv_ref[i], r.astype(BF16)).astype(BF16) for i, r in enumerate(rhs)]
        uv = [jnp.concatenate([u, stacked(v_ref, g, p)], axis=0) for (g, p), u in zip(chains, u_b)]
        y = [_dot(stacked(rt_ref, g, p), sb) + _dot(ar_ref[i], w)
             for i, ((g, p), sb, w) in enumerate(zip(chains, state_b, uv))]
        new_state = [_dot(bkt_ref[i], w) for i, w in enumerate(uv)]
        for i, (g, p) in enumerate(chains):
            sl = slice(p * LANES, (p + 1) * LANES)
            y_ref[g, :, sl] = y[i][0:C, :] + y[i][C:2 * C, :]
            s_ref[i] = state[i] * gcol_ref[i] + new_state[i]

    @pl.when(c_idx == 0)
    def _():
        s_ref[...] = jnp.zeros_like(s_ref)
        prepare(at_ref, rt_ref, bt_ref, kt_ref, bh_ref, kh_ref, v_ref, gc_ref)

    @pl.when(c_idx < last)
    def _():
        advance()
        prepare(at_nxt, rt_nxt, bt_nxt, kt_nxt, bh_nxt, kh_nxt, v_nxt, gc_nxt)

    @pl.when(c_idx == last)
    def _():
        advance()


def _rwkv_scan(rt, at, bt, kt, bh, kh, v, gc, batch, seq, nb):
    T, rdim = rt.shape
    C = SCAN_CHUNK
    nchunk = seq // C
    view = lambda t: t.reshape(batch, seq, rdim)
    nxt = lambda c: jnp.minimum(c + 1, nchunk - 1)
    slab_spec = pl.BlockSpec((nb, C, rdim), lambda b, c: (b, c, 0))
    next_spec = pl.BlockSpec((nb, C, rdim), lambda b, c: (b, nxt(c), 0))
    gc_spec = pl.BlockSpec((nb, 1, 1, rdim), lambda b, c: (b, c, 0, 0))
    gc_next = pl.BlockSpec((nb, 1, 1, rdim), lambda b, c: (b, nxt(c), 0, 0))
    nchains = nb * rdim // LANES
    slabs = [view(t) for t in (rt, at, bt, kt, bh, kh, v)]
    gc4 = gc.reshape(batch, nchunk, 1, rdim)
    y = pl.pallas_call(
        _scan_kernel,
        grid=(batch // nb, nchunk),
        in_specs=[slab_spec] * 7 + [gc_spec] + [next_spec] * 7 + [gc_next],
        out_specs=slab_spec,
        out_shape=jax.ShapeDtypeStruct((batch, seq, rdim), F32),
        scratch_shapes=[pltpu.VMEM((nchains, LANES, LANES), F32), pltpu.VMEM((nchains, LANES, LANES), BF16),
                        pltpu.VMEM((nchains, LANES, LANES), F32), pltpu.VMEM((nchains, LANES, 2 * LANES), BF16),
                        pltpu.VMEM((nchains, LANES, 2 * LANES), BF16), pltpu.VMEM((nchains, LANES, LANES), F32)],
        compiler_params=_params("arbitrary", "arbitrary"),
        name="rwkv_scan",
    )(*slabs, gc4, *slabs, gc4)
    return y.reshape(T, rdim)


def _mla_prep_kernel(x_ref, pos_ref, wm_ref, qg_ref, kvg_ref, wuq_ref, wuk_ref, wuv_ref, freq_ref, cplace_ref,
                     splace_ref, q_ref, k_ref, v_ref, *, q_lora, kv_lora, heads):
    tm = x_ref.shape[0]
    parts = [slice(i * (tm // 2), (i + 1) * (tm // 2)) for i in range(2)]

    def rms(t, gain):
        return t * lax.rsqrt(jnp.mean(t * t, axis=-1, keepdims=True) + RMS_EPS) * gain

    xb = [x_ref[p, :].astype(BF16) for p in parts]
    z_kv = [_dot(t, wm_ref[:, q_lora:]) for t in xb]
    z_q = [_dot(t, wm_ref[:, 0:q_lora]) for t in xb]
    kvn = [rms(t[:, 0:kv_lora], kvg_ref[...]).astype(BF16) for t in z_kv]
    kn = [_dot(t, wuk_ref[...]) for t in kvn]
    vt = [_dot_nt(wuv_ref[...], t) for t in kvn]
    qn = [rms(t, qg_ref[...]).astype(BF16) for t in z_q]
    qt = [_dot_nt(wuq_ref[...], t) for t in qn]

    q_scale = (HEAD_DIM + ROPE_DIM) ** -0.5 * math.log2(math.e)
    x1, x2 = slice(HEAD_DIM, HEAD_DIM + HALF_ROPE), slice(HEAD_DIM + HALF_ROPE, HEAD_DIM + ROPE_DIM)
    for i, p in enumerate(parts):
        v_ref[0, :, p] = vt[i].astype(BF16)
        ang_t = freq_ref[...] * pos_ref[0, :, p].astype(F32)
        cos_t, sin_t = jnp.cos(ang_t), jnp.sin(ang_t)
        for h in range(heads):
            blk = qt[i][h * LANES:(h + 1) * LANES, :]
            rot = jnp.concatenate([blk[0:HEAD_DIM, :], blk[x1, :] * cos_t - blk[x2, :] * sin_t,
                                   blk[x2, :] * cos_t + blk[x1, :] * sin_t, blk[HEAD_DIM + ROPE_DIM:, :]], axis=0)
            q_ref[0, h * LANES:(h + 1) * LANES, p] = (rot * q_scale).astype(BF16)
        cs = sum(_dot_tn(t, cplace_ref[...]) for t in _split3(cos_t))
        sn = sum(_dot_tn(t, splace_ref[...]) for t in _split3(sin_t))
        kpe = z_kv[i][:, kv_lora:kv_lora + LANES]
        kpe_rot = kpe * cs + pltpu.roll(kpe, LANES - HALF_ROPE, axis=1) * sn
        for h in range(heads):
            sl = slice(h * LANES, (h + 1) * LANES)
            k_ref[p, sl] = (kn[i][:, sl] + kpe_rot).astype(BF16)


def _mla_prep(x2, pos3, wm, qg, kvg, wuq, wuk, wuv, freq, cplace, splace, heads, tm):
    T, D = x2.shape
    q_lora, kv_lora = qg.shape[-1], kvg.shape[-1]
    row_spec = lambda w: pl.BlockSpec((tm, w), lambda i: (i, 0))
    tile_t = lambda w: pl.BlockSpec((1, w, tm), lambda i: (i, 0, 0))
    consts = (wm, qg, kvg, wuq, wuk, wuv, freq, cplace, splace)
    return pl.pallas_call(
        functools.partial(_mla_prep_kernel, q_lora=q_lora, kv_lora=kv_lora, heads=heads),
        grid=(T // tm,),
        in_specs=[row_spec(D), pl.BlockSpec((1, 1, tm), lambda i: (i, 0, 0))] + [_const_spec(c.shape) for c in consts],
        out_specs=(tile_t(heads * LANES), row_spec(heads * LANES), tile_t(heads * HEAD_DIM)),
        out_shape=(jax.ShapeDtypeStruct((T // tm, heads * LANES, tm), BF16),
                   jax.ShapeDtypeStruct((T, heads * LANES), BF16),
                   jax.ShapeDtypeStruct((T // tm, heads * HEAD_DIM, tm), BF16)),
        compiler_params=_params("arbitrary"),
        name="mla_prep",
    )(x2, pos3, *consts)


def _attn_kernel(q_ref, k_ref, vt_ref, o_ref, m_ref, l_ref, acc_ref, sa_ref, sb_ref):
    tk = vt_ref.shape[2]
    nq, _, tq = q_ref.shape
    nh = q_ref.shape[1] // LANES
    key_idx = lax.broadcasted_iota(jnp.int32, (tk, 1), 0)
    qry_idx = lax.broadcasted_iota(jnp.int32, (1, tq), 1)

    def reset():
        m_ref[...] = jnp.full_like(m_ref, NEG_BIG)
        l_ref[...] = jnp.zeros_like(l_ref)
        acc_ref[...] = jnp.zeros_like(acc_ref)

    def scores(qi, t, s_ref):
        qi = jnp.minimum(qi, nq - 1)
        kblk = k_ref[pl.ds(pl.multiple_of(t * tk, tk), tk), :]
        qt = q_ref[qi]
        for h in range(nh):
            sl = slice(h * LANES, (h + 1) * LANES)
            s_ref[h] = _dot(kblk[:, sl], qt[sl, :])

    def consume(t, s_ref, visible):
        vt = vt_ref[t]
        for h in range(nh):
            s = s_ref[h] if visible is None else jnp.where(visible, s_ref[h], NEG_BIG)
            m_old = m_ref[h]
            m_new = jnp.maximum(m_old, jnp.max(s, axis=0, keepdims=True))
            p = jnp.exp2(s - m_new)
            alpha = jnp.exp2(m_old - m_new)
            l_ref[h] = alpha * l_ref[h] + jnp.sum(p, axis=0, keepdims=True)
            m_ref[h] = m_new
            acc_ref[h] = alpha * acc_ref[h] + _dot(vt[h * HEAD_DIM:(h + 1) * HEAD_DIM, :], p.astype(BF16))

    def finish(qi):
        out_t = jnp.concatenate([acc_ref[h] / l_ref[h] for h in range(nh)], axis=0)
        o_ref[pl.ds(pl.multiple_of(qi * tq, tq), tq), :] = out_t.T.astype(o_ref.dtype)
        reset()

    def succ(qi, t):
        last = t == qi
        return jnp.where(last, qi + 1, qi), jnp.where(last, 0, t + 1)

    def close(qi, t, s_ref, diagonal):
        consume(t, s_ref, key_idx <= qry_idx if diagonal else None)
        if diagonal:
            finish(qi)

    reset()
    scores(0, 0, sa_ref)

    def body(j, carry):
        qi, t = carry
        q1, t1 = succ(qi, t)
        q2, t2 = succ(q1, t1)
        for d0 in (False, True):
            for d1 in (False, True):
                @pl.when(((t == qi) == d0) & ((t1 == q1) == d1))
                def _(d0=d0, d1=d1):
                    scores(q1, t1, sb_ref)
                    close(qi, t, sa_ref, d0)
                    scores(q2, t2, sa_ref)
                    close(q1, t1, sb_ref, d1)
        return q2, t2

    total = nq * (nq + 1) // 2
    lax.fori_loop(0, total // 2, body, (jnp.int32(0), jnp.int32(0)))
    if total % 2:
        close(jnp.int32(nq - 1), jnp.int32(nq - 1), sa_ref, True)


def _mla_attn(qt, k, vt, batch, seq, heads, nh):
    T = k.shape[0]
    tk = vt.shape[2]
    nk = seq // tk
    qt4 = qt.reshape(batch, nk, heads * LANES, tk)
    vt4 = vt.reshape(batch, nk, heads * HEAD_DIM, tk)
    return pl.pallas_call(
        _attn_kernel,
        grid=(batch, heads // nh),
        in_specs=[pl.BlockSpec((None, nk, nh * LANES, tk), lambda b, p: (b, 0, p, 0)),
                  pl.BlockSpec((seq, nh * LANES), lambda b, p: (b, p)),
                  pl.BlockSpec((None, nk, nh * HEAD_DIM, tk), lambda b, p: (b, 0, p, 0))],
        out_specs=pl.BlockSpec((seq, nh * HEAD_DIM), lambda b, p: (b, p)),
        out_shape=jax.ShapeDtypeStruct((T, heads * HEAD_DIM), BF16),
        scratch_shapes=[pltpu.VMEM((nh, 1, tk), F32), pltpu.VMEM((nh, 1, tk), F32),
                        pltpu.VMEM((nh, HEAD_DIM, tk), F32),
                        pltpu.VMEM((nh, tk, tk), F32), pltpu.VMEM((nh, tk, tk), F32)],
        compiler_params=_params("arbitrary", "arbitrary"),
        name="mla_attn",
    )(qt4, k, vt4)


def _layer_norm(t, gain, bias):
    mu = jnp.mean(t, axis=-1, keepdims=True)
    c = t - mu
    var = jnp.mean(c * c, axis=-1, keepdims=True)
    return c * lax.rsqrt(var + LN_EPS) * gain + bias


def _mix_kernel(x_ref, y_ref, bonus_ref, g_ref, ym_ref, wg_ref, wpr_ref, wpm_ref, wo_ref, lnxg_ref, lnxb_ref,
                ln1g_ref, ln1b_ref, h_ref, *, alpha):
    tm, d = x_ref.shape
    parts = [slice(0, tm // 2), slice(tm // 2, tm)]
    gates = [_dot(x_ref[p, :].astype(BF16), wg_ref[...]) for p in parts]
    pm = [_dot(ym_ref[p, :], wpm_ref[...]) for p in parts]
    lane = lax.broadcasted_iota(jnp.int32, (1, LANES), 1)
    low = lane < HEAD_DIM
    inv_n = 1.0 / HEAD_DIM

    def head_sums(t):
        s_all = jnp.sum(t, axis=1, keepdims=True)
        s_low = jnp.sum(jnp.where(low, t, 0.0), axis=1, keepdims=True)
        return jnp.where(low, s_low, s_all - s_low)

    def group_norm(t):
        cols = []
        for c in range(0, t.shape[1], LANES):
            blk = t[:, c:c + LANES]
            cen = blk - head_sums(blk) * inv_n
            cols.append(cen * lax.rsqrt(head_sums(cen * cen) * inv_n + GN_EPS))
        return jnp.concatenate(cols, axis=1)

    y_r = [((group_norm(y_ref[p, :]) * lnxg_ref[...] + lnxb_ref[...] + bonus_ref[p, :]) * g_ref[p, :]).astype(BF16)
           for p in parts]
    pr = [_dot(t, wpr_ref[...]) for t in y_r]
    merged = [(jax.nn.sigmoid(gt[:, 0:d]) * a + jax.nn.sigmoid(gt[:, d:2 * d]) * b).astype(BF16)
              for gt, a, b in zip(gates, pr, pm)]
    out = [_dot(t, wo_ref[...]) for t in merged]
    for p, o in zip(parts, out):
        h_ref[p, :] = _layer_norm(alpha * x_ref[p, :] + o, ln1g_ref[...], ln1b_ref[...])


def _mix(x2, y, bonus, g, ym, wg, wpr, wpm, wo, lnxg, lnxb, ln1g, ln1b, alpha, tm):
    T, D = x2.shape
    rdim = y.shape[1]
    row_spec = lambda w: pl.BlockSpec((tm, w), lambda i: (i, 0))
    consts = (wg, wpr, wpm, wo, lnxg, lnxb, ln1g, ln1b)
    return pl.pallas_call(
        functools.partial(_mix_kernel, alpha=alpha),
        grid=(T // tm,),
        in_specs=[row_spec(D), row_spec(rdim), row_spec(rdim), row_spec(rdim), row_spec(ym.shape[1])]
        + [_const_spec(c.shape) for c in consts],
        out_specs=row_spec(D),
        out_shape=jax.ShapeDtypeStruct((T, D), F32),
        compiler_params=_params("arbitrary"),
        name="mix",
    )(x2, y, bonus, g, ym, *consts)


def _ffn_kernel(h_ref, wg_ref, wu_ref, wd_ref, ln2g_ref, ln2b_ref, o_ref, *, alpha, ff_bounds):
    h = h_ref[...]
    hb = h.astype(BF16)
    acc = alpha * h
    chunks = list(zip(ff_bounds[:-1], ff_bounds[1:]))
    for lo, hi in chunks[:-1]:
        sl = slice(lo, hi)
        gate = _dot(hb, wg_ref[:, sl])
        act = (gate * jax.nn.sigmoid(gate) * _dot(hb, wu_ref[:, sl])).astype(BF16)
        acc = acc + _dot(act, wd_ref[sl, :])
    sl = slice(*chunks[-1])
    gate = _dot(hb, wg_ref[:, sl])
    act = (gate * jax.nn.sigmoid(gate) * _dot(hb, wu_ref[:, sl])).astype(BF16)
    tm = h.shape[0]
    th = min(tm, PART_ROWS)
    for r in range(0, tm, th):
        rows = slice(r, r + th)
        o_ref[rows, :] = _layer_norm(acc[rows, :] + _dot(act[rows, :], wd_ref[sl, :]), ln2g_ref[...], ln2b_ref[...])


def _ffn(h, wg, wu, wd, ln2g, ln2b, alpha, tm):
    T, D = h.shape
    dff = wg.shape[1]
    nchunks = max(1, round(dff * tm / FF_CHUNK_ELEMS))
    width = -(-dff // (nchunks * MXU_TILE)) * MXU_TILE
    ff_bounds = tuple(range(0, dff, width)) + (dff,)
    row_spec = pl.BlockSpec((tm, D), lambda i: (i, 0))
    consts = (wg, wu, wd, ln2g, ln2b)
    return pl.pallas_call(
        functools.partial(_ffn_kernel, alpha=alpha, ff_bounds=ff_bounds),
        grid=(T // tm,),
        in_specs=[row_spec] + [_const_spec(c.shape) for c in consts],
        out_specs=row_spec,
        out_shape=jax.ShapeDtypeStruct((T, D), F32),
        compiler_params=_params("arbitrary"),
        name="ffn",
    )(h, *consts)


def _layer(x2, pos2, batch, seq, alpha, w_in, mu_shift, w_decay_up, w_decay_base, w_aaa_up, w_aaa_base, w_gate_up,
           k_k, k_a, r_k, lnx_g, lnx_b, q_norm_g, w_uq, kv_norm_g, w_ukv, w_proj_rwkv, w_proj_mla, w_out,
           ln1_g, ln1_b, w_ffn_gate, w_ffn_up, w_ffn_down, ln2_g, ln2_b):
    D = x2.shape[1]
    rdim = k_k.shape[0]
    dl, al, gl = w_decay_up.shape[0], w_aaa_up.shape[0], w_gate_up.shape[0]
    q_lora, kv_lora = q_norm_g.shape[0], kv_norm_g.shape[0]
    heads = w_uq.shape[1] // (HEAD_DIM + ROPE_DIM)
    assert rdim % LANES == 0 and dl + al == LANES and gl == LANES and heads % 2 == 0
    assert w_ukv.shape[1] == heads * 2 * HEAD_DIM and w_proj_mla.shape[0] == heads * HEAD_DIM
    shift_cols = 3 * rdim + dl + al + gl
    tm = min(512, seq)
    assert seq % tm == 0 and tm % SCAN_CHUNK == 0
    row = lambda t: t.reshape(1, -1).astype(F32)

    ws = w_in[:, :shift_cols].astype(BF16)
    wdu = jnp.concatenate([w_decay_up, jnp.zeros((al, rdim), F32)], axis=0).astype(BF16)
    wau = jnp.concatenate([jnp.zeros((dl, rdim), F32), w_aaa_up], axis=0).astype(BF16)
    assert rdim % MXU_TILE == 0
    hid = jnp.arange(MXU_TILE) // HEAD_DIM
    hsum = (hid[:, None] == hid[None, :]).astype(BF16)
    rt, at, bt, kt, bh, kh, v, gc, bonus, g = _rwkv_prep(
        x2, ws, row(mu_shift), wdu, wau, w_gate_up.astype(BF16), row(w_decay_base), row(w_aaa_base), row(k_k),
        row(k_a), row(r_k), hsum, seq, tm)
    y = _rwkv_scan(rt, at, bt, kt, bh, kh, v, gc, batch, seq, 4 if batch % 4 == 0 else 1)

    o = shift_cols
    w_cq, w_ckv = w_in[:, o:o + q_lora], w_in[:, o + q_lora:o + q_lora + kv_lora]
    w_kpe = w_in[:, o + q_lora + kv_lora:o + q_lora + kv_lora + ROPE_DIM]
    pad_r = LANES - HEAD_DIM - ROPE_DIM
    wm = jnp.concatenate([w_cq, w_ckv, jnp.zeros((D, HEAD_DIM), F32), w_kpe, w_kpe[:, :HALF_ROPE],
                          jnp.zeros((D, pad_r - HALF_ROPE), F32)], axis=1).astype(BF16)
    wuq = jnp.pad(w_uq.reshape(q_lora, heads, HEAD_DIM + ROPE_DIM), ((0, 0), (0, 0), (0, pad_r)))
    wuq = wuq.reshape(q_lora, heads * LANES).T.astype(BF16)
    ukv = w_ukv.reshape(kv_lora, heads, 2 * HEAD_DIM)
    wuk = jnp.pad(ukv[:, :, :HEAD_DIM], ((0, 0), (0, 0), (0, LANES - HEAD_DIM))).reshape(kv_lora, heads * LANES)
    wuv = ukv[:, :, HEAD_DIM:].reshape(kv_lora, heads * HEAD_DIM).T
    inv_freq = ROPE_THETA ** (-jnp.arange(0, ROPE_DIM, 2, dtype=F32) / ROPE_DIM)
    lane = jnp.arange(LANES)[None, :]
    j = jnp.arange(HALF_ROPE)[:, None]
    on_x1, on_x2 = lane == HEAD_DIM + j, lane == HEAD_DIM + HALF_ROPE + j
    cplace = (on_x1 | on_x2).astype(BF16)
    splace = (on_x2.astype(F32) - on_x1.astype(F32)).astype(BF16)
    q, k, vv = _mla_prep(x2, pos2.reshape(-1, 1, tm), wm, row(q_norm_g), row(kv_norm_g), wuq, wuk.astype(BF16),
                         wuv.astype(BF16), inv_freq.reshape(HALF_ROPE, 1), cplace, splace, heads, tm)
    ym = _mla_attn(q, k, vv, batch, seq, heads, 4 if heads % 4 == 0 else 2)

    o2 = o + q_lora + kv_lora + ROPE_DIM
    wg = w_in[:, o2:o2 + 2 * D].astype(BF16)
    h = _mix(x2, y, bonus, g, ym, wg, w_proj_rwkv.astype(BF16), w_proj_mla.astype(BF16), w_out.astype(BF16),
             row(lnx_g), row(lnx_b), row(ln1_g), row(ln1_b), alpha, tm)
    return _ffn(h, w_ffn_gate.astype(BF16), w_ffn_up.astype(BF16), w_ffn_down.astype(BF16), row(ln2_g), row(ln2_b),
                alpha, tm)


def kernel(x, positions, w_in, mu_shift, w_decay_up, w_decay_base, w_aaa_up, w_aaa_base, w_gate_up, k_k, k_a, r_k,
           lnx_g, lnx_b, q_norm_g, w_uq, kv_norm_g, w_ukv, w_proj_rwkv, w_proj_mla, w_out, ln1_g, ln1_b,
           w_ffn_gate, w_ffn_up, w_ffn_down, ln2_g, ln2_b):
    batch, seq, D = x.shape
    depth = w_in.shape[0]
    alpha = (2.0 * depth) ** 0.25
    h = x.reshape(batch * seq, D)
    pos2 = positions.reshape(batch * seq, 1)
    per_layer = (w_in, mu_shift, w_decay_up, w_decay_base, w_aaa_up, w_aaa_base, w_gate_up, k_k, k_a, r_k, lnx_g,
                 lnx_b, q_norm_g, w_uq, kv_norm_g, w_ukv, w_proj_rwkv, w_proj_mla, w_out, ln1_g, ln1_b, w_ffn_gate,
                 w_ffn_up, w_ffn_down, ln2_g, ln2_b)
    for l in range(depth):
        layer_params = (p.reshape(p.shape[1:]) if depth == 1 else p[l] for p in per_layer)
        h = _layer(h, pos2, batch, seq, alpha, *layer_params)
    return h.reshape(batch, seq, D)
```
